```python
import jax
import jax.numpy as jnp
from jax import lax
import numpy as np

D_MODEL = 2048
BATCH = 2
SEQ = 16384
DEPTH = 1

HEAD_DIM = 64
MIX_WIDTH = D_MODEL
ATTN_WIDTH = MIX_WIDTH // 2
RWKV_WIDTH = MIX_WIDTH - ATTN_WIDTH
ATTN_HEADS = ATTN_WIDTH // HEAD_DIM
RWKV_HEADS = RWKV_WIDTH // HEAD_DIM
WINDOWS = (128, 512, 2048)
DILATIONS = (1, 4, 16)
ROT_DIM = HEAD_DIM // 4
ROPE_THETA = 500000.0
DECAY_LORA = 64
ICLR_LORA = 64
GATE_LORA = 160
RWKV_COLS = 3 * RWKV_WIDTH + DECAY_LORA + ICLR_LORA + GATE_LORA
IN_COLS = 3 * ATTN_WIDTH + RWKV_COLS
FFN_DIM = -(-(8 * D_MODEL) // (3 * 256)) * 256
RMS_EPS = 1e-6
GN_EPS = HEAD_DIM * 1e-5

kernel_name = 'hymba_dilated_rwkv7_adaln_layer'


def _rmsnorm(x, g):
    xf = x.astype(jnp.float32)
    y = xf * lax.rsqrt(jnp.mean(xf * xf, axis=-1, keepdims=True) + RMS_EPS)
    return (y * g.astype(jnp.float32)).astype(x.dtype)


def _partial_rope(t, positions):
    inv = ROPE_THETA ** (-jnp.arange(0, ROT_DIM, 2, dtype=jnp.float32) / ROT_DIM)
    ang = positions.astype(jnp.float32)[..., None] * inv
    cos = jnp.cos(ang)[:, :, None, :]
    sin = jnp.sin(ang)[:, :, None, :]
    half = ROT_DIM // 2
    r1 = t[..., :half].astype(jnp.float32)
    r2 = t[..., half:ROT_DIM].astype(jnp.float32)
    rot = jnp.concatenate([r1 * cos - r2 * sin, r2 * cos + r1 * sin], axis=-1).astype(t.dtype)
    return jnp.concatenate([rot, t[..., ROT_DIM:]], axis=-1)


def _dilated_branch(q, k, v, window, dilation):
    b, s, h, dh = q.shape
    blk = window // dilation
    unit = blk * dilation
    s_pad = -(-s // unit) * unit
    n_sub = s_pad // dilation
    nb = n_sub // blk

    def to_blocks(t):
        t = jnp.pad(t, ((0, 0), (0, s_pad - s), (0, 0), (0, 0)))
        t = t.reshape(b, n_sub, dilation, h, dh).transpose(0, 3, 2, 1, 4)
        return t.reshape(b, h, dilation, nb, blk, dh)

    def with_prev(t):
        prev = jnp.pad(t, ((0, 0), (0, 0), (0, 0), (1, 0), (0, 0), (0, 0)))[:, :, :, :-1]
        return jnp.concatenate([prev, t], axis=4)

    qb = to_blocks(q)
    kw = with_prev(to_blocks(k))
    vw = with_prev(to_blocks(v))
    scores = jnp.einsum('bhrnqd,bhrnkd->bhrnqk', qb, kw,
                        preferred_element_type=jnp.float32) * (dh ** -0.5)
    qi = jnp.arange(blk)[:, None]
    kj = jnp.arange(2 * blk)[None, :]
    steps = blk + qi - kj
    band = (steps >= 0) & (steps <= blk)
    valid = band[None] & ((jnp.arange(nb)[:, None, None] > 0) | (kj[None] >= blk))
    scores = jnp.where(valid, scores, -jnp.inf)
    m = jnp.max(scores, axis=-1, keepdims=True)
    p = jnp.exp(scores - m)
    l = jnp.sum(p, axis=-1, keepdims=True)
    o = jnp.einsum('bhrnqk,bhrnkd->bhrnqd', p, vw.astype(jnp.float32)) / l
    lse = (m + jnp.log(l))[..., 0]
    o = o.reshape(b, h, dilation, n_sub, dh).transpose(0, 3, 2, 1, 4).reshape(b, s_pad, h, dh)[:, :s]
    lse = lse.reshape(b, h, dilation, n_sub).transpose(0, 3, 2, 1).reshape(b, s_pad, h)[:, :s]
    return o, lse


def _dilated_mixture(q, k, v):
    outs, lses = zip(*[_dilated_branch(q, k, v, w, d) for w, d in zip(WINDOWS, DILATIONS)])
    wts = jax.nn.softmax(jnp.stack(lses, axis=0), axis=0)
    o = jnp.sum(wts[..., None] * jnp.stack(outs, axis=0), axis=0)
    return o.astype(q.dtype)


def _rwkv7(y, mu_shift, w0, w_decay_up, a0, w_iclr_up, w_gate_up, k_k, k_a, r_k, gn_g, gn_b):
    dt = y.dtype
    b, s, _ = y.shape
    yf = y.astype(jnp.float32)
    prev = jnp.pad(yf, ((0, 0), (1, 0), (0, 0)))[:, :-1]
    yf = yf + (prev - yf) * mu_shift
    c0 = RWKV_WIDTH
    r = yf[..., :c0]
    k = yf[..., c0:2 * c0]
    v = yf[..., 2 * c0:3 * c0]
    wl = yf[..., 3 * c0:3 * c0 + DECAY_LORA]
    al = yf[..., 3 * c0 + DECAY_LORA:3 * c0 + DECAY_LORA + ICLR_LORA]
    gl = yf[..., 3 * c0 + DECAY_LORA + ICLR_LORA:]
    w = -jax.nn.softplus(-(w0 + jnp.tanh(wl) @ w_decay_up)) - 0.5
    decay = jnp.exp(-jnp.exp(w))
    a = jax.nn.sigmoid(a0 + al @ w_iclr_up)
    g = jax.nn.sigmoid(gl) @ w_gate_up
    hd = lambda t: t.reshape(b, s, RWKV_HEADS, HEAD_DIM)
    kk = hd(k * k_k)
    kk = kk / jnp.maximum(jnp.sqrt(jnp.sum(kk * kk, axis=-1, keepdims=True)), 1e-12)
    k = k * (1.0 + (a - 1.0) * k_a)
    r_h, w_h, k_h, v_h, a_h = hd(r), hd(decay), hd(k), hd(v), hd(a)

    def step(state, inp):
        r_t, w_t, k_t, v_t, kk_t, a_t = inp
        sa = jnp.einsum('bhvk,bhk->bhv', state, -kk_t)
        state = (state * w_t[..., None, :]
                 + sa[..., :, None] * (kk_t * a_t)[..., None, :]
                 + v_t[..., :, None] * k_t[..., None, :])
        return state, jnp.einsum('bhvk,bhk->bhv', state, r_t)

    xs = tuple(jnp.moveaxis(t, 1, 0) for t in (r_h, w_h, k_h, v_h, kk, a_h))
    state0 = jnp.zeros((b, RWKV_HEADS, HEAD_DIM, HEAD_DIM), jnp.float32)
    _, outs = lax.scan(step, state0, xs)
    o = jnp.moveaxis(outs, 0, 1)
    mean = jnp.mean(o, axis=-1, keepdims=True)
    var = jnp.mean(jnp.square(o - mean), axis=-1, keepdims=True)
    o = ((o - mean) * lax.rsqrt(var + GN_EPS)).reshape(b, s, RWKV_WIDTH) * gn_g + gn_b
    bonus = jnp.sum(r_h * k_h * r_k, axis=-1, keepdims=True) * v_h
    o = (o + bonus.reshape(b, s, RWKV_WIDTH)) * g
    return o.astype(dt)


def _swiglu(h, w_gate, w_up, w_down):
    return (jax.nn.silu(h @ w_gate) * (h @ w_up)) @ w_down


def setup_inputs(seed: int = 0) -> dict:
    key = jax.random.key(seed)
    ks = jax.random.split(key, 26)
    f32 = jnp.float32
    L = DEPTH

    def nrm(k, shape, scale):
        return jax.random.normal(k, shape, f32) * scale

    return {
        'x': nrm(ks[0], (BATCH, SEQ, D_MODEL), 1.0),
        'c': nrm(ks[1], (BATCH, D_MODEL), 1.0),
        'positions': jnp.broadcast_to(jnp.arange(SEQ, dtype=jnp.int32), (BATCH, SEQ)),
        'w_ada': nrm(ks[2], (L, D_MODEL, 6 * D_MODEL), 0.5 * D_MODEL ** -0.5),
        'b_ada': nrm(ks[3], (L, 6 * D_MODEL), 0.01),
        'norm1_g': 1.0 + nrm(ks[4], (L, D_MODEL), 0.05),
        'norm2_g': 1.0 + nrm(ks[5], (L, D_MODEL), 0.05),
        'normf_g': 1.0 + nrm(ks[6], (D_MODEL,), 0.05),
        'w_in': nrm(ks[7], (L, D_MODEL, IN_COLS), D_MODEL ** -0.5),
        'w_out': nrm(ks[8], (L, MIX_WIDTH, D_MODEL), MIX_WIDTH ** -0.5),
        'mu_shift': jax.random.uniform(ks[9], (L, RWKV_COLS), f32, 0.0, 1.0),
        'w0': jax.random.uniform(ks[10], (L, RWKV_WIDTH), f32, -6.0, 1.0),
        'w_decay_up': nrm(ks[11], (L, DECAY_LORA, RWKV_WIDTH), 0.5 * DECAY_LORA ** -0.5),
        'a0': nrm(ks[12], (L, RWKV_WIDTH), 0.1),
        'w_iclr_up': nrm(ks[13], (L, ICLR_LORA, RWKV_WIDTH), 0.5 * ICLR_LORA ** -0.5),
        'w_gate_up': nrm(ks[14], (L, GATE_LORA, RWKV_WIDTH), GATE_LORA ** -0.5),
        'k_k': 0.85 + nrm(ks[15], (L, RWKV_WIDTH), 0.05),
        'k_a': 1.0 + nrm(ks[16], (L, RWKV_WIDTH), 0.05),
        'r_k': nrm(ks[17], (L, RWKV_HEADS, HEAD_DIM), 0.1),
        'gn_g': 1.0 + nrm(ks[18], (L, RWKV_WIDTH), 0.05),
        'gn_b': nrm(ks[19], (L, RWKV_WIDTH), 0.01),
        'w_ffn_gate': nrm(ks[20], (L, D_MODEL, FFN_DIM), D_MODEL ** -0.5),
        'w_ffn_up': nrm(ks[21], (L, D_MODEL, FFN_DIM), D_MODEL ** -0.5),
        'w_ffn_down': nrm(ks[22], (L, FFN_DIM, D_MODEL), FFN_DIM ** -0.5),
    }


def reference(x, c, positions, w_ada, b_ada, norm1_g, norm2_g, normf_g, w_in, w_out,
              mu_shift, w0, w_decay_up, a0, w_iclr_up, w_gate_up, k_k, k_a, r_k,
              gn_g, gn_b, w_ffn_gate, w_ffn_up, w_ffn_down):
    b, s, _ = x.shape
    for i in range(DEPTH):
        ada = (jax.nn.silu(c) @ w_ada[i] + b_ada[i])[:, None, :]
        sh1, sc1, gt1, sh2, sc2, gt2 = jnp.split(ada, 6, axis=-1)
        h = _rmsnorm(x, norm1_g[i]) * (1.0 + sc1) + sh1
        proj = h @ w_in[i]
        q = proj[..., :ATTN_WIDTH].reshape(b, s, ATTN_HEADS, HEAD_DIM)
        k = proj[..., ATTN_WIDTH:2 * ATTN_WIDTH].reshape(b, s, ATTN_HEADS, HEAD_DIM)
        v = proj[..., 2 * ATTN_WIDTH:3 * ATTN_WIDTH].reshape(b, s, ATTN_HEADS, HEAD_DIM)
        q = _partial_rope(q, positions)
        k = _partial_rope(k, positions)
        attn = _dilated_mixture(q, k, v).reshape(b, s, ATTN_WIDTH)
        rwkv = _rwkv7(proj[..., 3 * ATTN_WIDTH:], mu_shift[i], w0[i], w_decay_up[i], a0[i],
                      w_iclr_up[i], w_gate_up[i], k_k[i], k_a[i], r_k[i], gn_g[i], gn_b[i])
        mix = jnp.concatenate([attn, rwkv], axis=-1) @ w_out[i]
        x = x + gt1 * mix
        h2 = _rmsnorm(x, norm2_g[i]) * (1.0 + sc2) + sh2
        x = x + gt2 * _swiglu(h2, w_ffn_gate[i], w_ffn_up[i], w_ffn_down[i])
    return _rmsnorm(x, normf_g)
```

```python
import functools
import math

import jax
import jax.numpy as jnp
from jax import lax
from jax.experimental import pallas as pl
from jax.experimental.pallas import tpu as pltpu

HEAD_DIM = 64
LANES = 128
ROT_DIM = HEAD_DIM // 4
ROPE_THETA = 500000.0
RMS_EPS = 1e-6
GN_EPS = HEAD_DIM * 1e-5
DILATIONS = (1, 4, 16)
WINDOW_BLOCK = 128
SPAN = WINDOW_BLOCK * max(DILATIONS)
DECAY_LORA = 64
ICLR_LORA = 64
GATE_LORA = 160
LORA_PAD = 512
CHUNK = 64
NEG = -1e30
VMEM_LIMIT = 56 * 1024 * 1024

F32 = jnp.float32
BF16 = jnp.bfloat16
HI = lax.Precision.HIGHEST


def _cparams(sem):
    return pltpu.CompilerParams(dimension_semantics=sem, vmem_limit_bytes=VMEM_LIMIT)


def _sigmoid(z):
    return 1.0 / (1.0 + jnp.exp(-z))


def _ada_kernel(c_ref, w_ref, b_ref, o_ref):
    c = c_ref[...]
    s = c * _sigmoid(c)
    o_ref[...] = jnp.dot(s, w_ref[...], precision=HI, preferred_element_type=F32) + b_ref[...]


def _ada(c8, w_ada, b_ada):
    rows, d = c8.shape
    n = w_ada.shape[1]
    tn = _largest_tile(n, 1024, LANES)
    return pl.pallas_call(
        _ada_kernel,
        grid=(n // tn,),
        in_specs=[
            pl.BlockSpec((rows, d), lambda j: (0, 0)),
            pl.BlockSpec((d, tn), lambda j: (0, j)),
            pl.BlockSpec((1, tn), lambda j: (0, j)),
        ],
        out_specs=pl.BlockSpec((rows, tn), lambda j: (0, j)),
        out_shape=jax.ShapeDtypeStruct((rows, n), F32),
        compiler_params=_cparams(("arbitrary",)),
        name="ada",
    )(c8, w_ada, b_ada)


def _inproj_kernel(x_ref, pos_ref, sc_ref, sh_ref, g_ref, w_ref, wl_ref, mu_ref, mul_ref,
                   p_ref, l_ref, h_scr, cos_scr, sa_scr, sb_scr, last_scr, lastl_scr,
                   *, tiles_per_batch, n_attn_tiles):
    i = pl.program_id(0)
    j = pl.program_id(1)
    tm, tn = p_ref.shape
    first_in_batch = (i % tiles_per_batch) == 0

    @pl.when(j == 0)
    def _():
        x = x_ref[...]
        ms = jnp.mean(x * x, axis=-1, keepdims=True)
        y = x * lax.rsqrt(ms + RMS_EPS) * g_ref[...]
        h = y * (1.0 + sc_ref[...]) + sh_ref[...]
        h_scr[...] = h.astype(BF16)
        lane = lax.broadcasted_iota(jnp.int32, (1, LANES), 1)
        freq = (lane % (ROT_DIM // 2)).astype(F32)
        inv = jnp.exp(freq * (-2.0 / ROT_DIM * math.log(ROPE_THETA)))
        ang = pos_ref[...] * inv
        cs = jnp.cos(ang)
        sn = jnp.sin(ang)
        lm = lane % HEAD_DIM
        cos_scr[...] = jnp.where(lm < ROT_DIM, cs, 1.0)
        sa_scr[...] = jnp.where(lm < ROT_DIM // 2, -sn, 0.0)
        sb_scr[...] = jnp.where((lm >= ROT_DIM // 2) & (lm < ROT_DIM), sn, 0.0)

    res = jnp.dot(h_scr[...], w_ref[...], preferred_element_type=F32)

    def shifted(val, carry_row, mu):
        prev = pltpu.roll(val, 1, 0)
        row = lax.broadcasted_iota(jnp.int32, val.shape, 0)
        carry = jnp.where(first_in_batch, 0.0, carry_row)
        prev = jnp.where(row == 0, carry, prev)
        return val + (prev - val) * mu

    @pl.when(j < 2)
    def _():
        ct = cos_scr[...]
        sa = sa_scr[...]
        sb = sb_scr[...]
        for gidx in range(tn // LANES):
            cols = slice(gidx * LANES, (gidx + 1) * LANES)
            t = res[:, cols]
            up = pltpu.roll(t, LANES - ROT_DIM // 2, 1)
            dn = pltpu.roll(t, ROT_DIM // 2, 1)
            p_ref[:, cols] = t * ct + up * sa + dn * sb

    @pl.when(j == 2)
    def _():
        p_ref[...] = res

    @pl.when(j >= n_attn_tiles)
    def _():
        jj = j - n_attn_tiles
        carry_row = last_scr[jj, 7:8, :]
        p_ref[...] = shifted(res, carry_row, mu_ref[...])
        last_scr[jj] = res[tm - 8:, :]

    @pl.when(j == n_attn_tiles)
    def _():
        resl = jnp.dot(h_scr[...], wl_ref[...], preferred_element_type=F32)
        y = shifted(resl, lastl_scr[7:8, :], mul_ref[...])
        lastl_scr[...] = resl[tm - 8:, :]
        l_ref[:, 0:LANES] = jnp.tanh(y[:, 0:LANES])
        l_ref[:, LANES:2 * LANES] = y[:, LANES:2 * LANES]
        l_ref[:, 2 * LANES:] = _sigmoid(y[:, 2 * LANES:])


def _inproj(x2, posb, sc1, sh1, g1, w_main, w_lora, mu_main, mu_lora, *, seq, tm):
    m, d = x2.shape
    n = w_main.shape[1]
    tn = n // 6
    n_attn_tiles = 3
    tiles_per_batch = seq // tm
    kern = functools.partial(_inproj_kernel, tiles_per_batch=tiles_per_batch,
                             n_attn_tiles=n_attn_tiles)
    return pl.pallas_call(
        kern,
        grid=(m // tm, 6),
        in_specs=[
            pl.BlockSpec((tm, d), lambda i, j: (i, 0)),
            pl.BlockSpec((tm, LANES), lambda i, j: (i, 0)),
            pl.BlockSpec((None, 1, d), lambda i, j: (i // tiles_per_batch, 0, 0)),
            pl.BlockSpec((None, 1, d), lambda i, j: (i // tiles_per_batch, 0, 0)),
            pl.BlockSpec((1, d), lambda i, j: (0, 0)),
            pl.BlockSpec((d, tn), lambda i, j: (0, j)),
            pl.BlockSpec((d, LORA_PAD), lambda i, j: (0, 0)),
            pl.BlockSpec((1, tn), lambda i, j: (0, jnp.maximum(j - n_attn_tiles, 0))),
            pl.BlockSpec((1, LORA_PAD), lambda i, j: (0, 0)),
        ],
        out_specs=[
            pl.BlockSpec((tm, tn), lambda i, j: (i, j)),
            pl.BlockSpec((tm, LORA_PAD), lambda i, j: (i, 0)),
        ],
        out_shape=[
            jax.ShapeDtypeStruct((m, n), F32),
            jax.ShapeDtypeStruct((m, LORA_PAD), F32),
        ],
        scratch_shapes=[
            pltpu.VMEM((tm, d), BF16),
            pltpu.VMEM((tm, LANES), F32),
            pltpu.VMEM((tm, LANES), F32),
            pltpu.VMEM((tm, LANES), F32),
            pltpu.VMEM((3, 8, tn), F32),
            pltpu.VMEM((8, LORA_PAD), F32),
        ],
        compiler_params=_cparams(("arbitrary", "arbitrary")),
        name="inproj",
    )(x2, posb, sc1, sh1, g1, w_main, w_lora, mu_main, mu_lora)


def _attn_kernel(q_ref, kc_ref, kp_ref, vc_ref, vp_ref, o_ref, pv_scr, m_scr, l_scr):
    blk = WINDOW_BLOCK
    has_prev = pl.program_id(2) > 0
    lane = lax.broadcasted_iota(jnp.int32, (1, LANES), 1)
    head0 = lane < HEAD_DIM
    qi = lax.broadcasted_iota(jnp.int32, (blk, 2 * blk), 0)
    kj = lax.broadcasted_iota(jnp.int32, (blk, 2 * blk), 1)
    in_cur = kj >= blk
    slack = jnp.where(in_cur, qi - (kj - blk), kj - qi)
    bias_full = jnp.where(slack >= 0, 0.0, NEG)
    bias_cur = jnp.where(in_cur, bias_full, NEG)
    bias_edge = jnp.where(has_prev, bias_full, bias_cur)

    def block(d, q_start, prev_ref, prev_start, bias):
        def rows(start):
            return pl.ds(start, blk, stride=d) if d > 1 else pl.ds(start, blk)
        q = q_ref[rows(q_start), :] * (HEAD_DIM ** -0.5)
        kwin = jnp.concatenate([prev_ref[0][rows(prev_start), :], kc_ref[rows(q_start), :]],
                               axis=0).astype(BF16)
        vwin = jnp.concatenate([prev_ref[1][rows(prev_start), :], vc_ref[rows(q_start), :]],
                               axis=0).astype(BF16)
        outs = []
        for h in range(2):
            mh = head0 if h == 0 else jnp.logical_not(head0)
            qh = jnp.where(mh, q, 0.0).astype(BF16)
            s = lax.dot_general(qh, kwin, (((1,), (1,)), ((), ())),
                                preferred_element_type=F32) + bias
            mx = jnp.max(s, axis=-1, keepdims=True)
            p = jnp.exp(s - mx)
            den = jnp.sum(p, axis=-1, keepdims=True)
            pv = jnp.dot(p.astype(BF16), vwin, preferred_element_type=F32)
            outs.append((pv, mx, den))
        pv = jnp.where(head0, outs[0][0], outs[1][0])
        mx = jnp.where(head0, outs[0][1], outs[1][1])
        den = jnp.where(head0, outs[0][2], outs[1][2])
        return pv, mx, den, rows(q_start)

    cur = (kc_ref, vc_ref)
    prv = (kp_ref, vp_ref)

    for slot, d in ((0, DILATIONS[1]), (1, DILATIONS[2])):
        unit = blk * d
        n_units = SPAN // unit

        def store(res, slot=slot):
            pv, mx, den, rr = res
            pv_scr[slot, rr, :] = pv
            m_scr[slot, rr, :] = mx
            l_scr[slot, rr, :] = den

        def edge_body(r, _, d=d, unit=unit, n_units=n_units, store=store):
            store(block(d, r, prv, (n_units - 1) * unit + r, bias_edge))
            return 0
        lax.fori_loop(0, d, edge_body, 0)

        if n_units > 1:
            def inner_body(t, _, d=d, unit=unit, store=store):
                n = t // d + 1
                r = t % d
                start = n * unit + r
                store(block(d, start, cur, start - unit, bias_full))
                return 0
            lax.fori_loop(0, (n_units - 1) * d, inner_body, 0)

    def finish(res):
        pv, mx, den, rr = res
        m4, m16 = m_scr[0, rr, :], m_scr[1, rr, :]
        mm = jnp.maximum(mx, jnp.maximum(m4, m16))
        e1, e4, e16 = jnp.exp(mx - mm), jnp.exp(m4 - mm), jnp.exp(m16 - mm)
        num = e1 * pv + e4 * pv_scr[0, rr, :] + e16 * pv_scr[1, rr, :]
        dn = e1 * den + e4 * l_scr[0, rr, :] + e16 * l_scr[1, rr, :]
        o_ref[rr, :] = (num / dn).astype(o_ref.dtype)

    finish(block(1, 0, prv, SPAN - blk, bias_edge))

    def dense_body(n, _):
        start = pl.multiple_of(n * blk, blk)
        finish(block(1, start, cur, start - blk, bias_full))
        return 0
    lax.fori_loop(1, SPAN // blk, dense_body, 0)


def _attention(p3, *, n_pairs):
    b, s, _ = p3.shape
    n_spans = s // SPAN

    def spec(col0, prev):
        if prev:
            return pl.BlockSpec((None, SPAN, LANES),
                                lambda bi, hp, si: (bi, jnp.maximum(si - 1, 0), col0 + hp))
        return pl.BlockSpec((None, SPAN, LANES), lambda bi, hp, si: (bi, si, col0 + hp))

    return pl.pallas_call(
        _attn_kernel,
        grid=(b, n_pairs, n_spans),
        in_specs=[spec(0, False), spec(n_pairs, False), spec(n_pairs, True),
                  spec(2 * n_pairs, False), spec(2 * n_pairs, True)],
        out_specs=pl.BlockSpec((None, SPAN, LANES), lambda bi, hp, si: (bi, si, hp)),
        out_shape=jax.ShapeDtypeStruct((b, s, n_pairs * LANES), BF16),
        scratch_shapes=[pltpu.VMEM((2, SPAN, LANES), F32)] * 3,
        compiler_params=_cparams(("arbitrary", "arbitrary", "arbitrary")),
        name="attn",
    )(p3, p3, p3, p3, p3)


def _unit_lower_inverse(a, prec):
    n = a.shape[0]
    ri = lax.broadcasted_iota(jnp.int32, (n, n), 0)
    ci = lax.broadcasted_iota(jnp.int32, (n, n), 1)
    eye = (ri == ci).astype(F32)
    mm = lambda x, y: jnp.dot(x, y, precision=prec, preferred_element_type=F32)
    base = 8
    ad = jnp.where(ri // base == ci // base, a, 0.0)
    a2 = mm(ad, ad)
    a4 = mm(a2, a2)
    p1 = eye + ad + a2 + mm(ad, a2)
    t = p1 + mm(p1, a4)
    size = base
    while size < n:
        off = jnp.where((ri // (2 * size) == ci // (2 * size)) & (ri // size != ci // size), a, 0.0)
        t = t + mm(t, mm(off, t))
        size *= 2
    return t


def _rwkv_kernel(r_ref, k_ref, v_ref, l_ref, wd_ref, wa_ref, wg_ref, cp_ref, o_ref, st_scr,
                 *, n_chunks):
    @pl.when(pl.program_id(2) == 0)
    def _():
        st_scr[...] = jnp.zeros_like(st_scr)

    c = CHUNK
    lane = lax.broadcasted_iota(jnp.int32, (1, LANES), 1)
    head0 = lane < HEAD_DIM
    r2 = lax.broadcasted_iota(jnp.int32, (LANES, LANES), 0)
    c2 = lax.broadcasted_iota(jnp.int32, (LANES, LANES), 1)
    same_head = (r2 // HEAD_DIM) == (c2 // HEAD_DIM)
    eye2 = r2 == c2
    ri = lax.broadcasted_iota(jnp.int32, (c, c), 0)
    ci = lax.broadcasted_iota(jnp.int32, (c, c), 1)
    strict = ci < ri
    incl = ci <= ri
    tri = incl.astype(F32)

    cp = cp_ref[...]
    w0, a0, k_k, k_a, r_k, gn_g, gn_b = (cp[n:n + 1, :] for n in range(7))
    wd = wd_ref[...]
    wa = wa_ref[...]
    wg = wg_ref[...]

    def head_sum(x):
        s0 = jnp.sum(jnp.where(head0, x, 0.0), axis=-1, keepdims=True)
        s1 = jnp.sum(jnp.where(head0, 0.0, x), axis=-1, keepdims=True)
        return jnp.where(head0, s0, s1)

    def bdot(x, y):
        return jnp.dot(x.astype(BF16), y.astype(BF16), preferred_element_type=F32)

    def tdot(x, y):
        return lax.dot_general(x.astype(BF16), y.astype(BF16), (((0,), (0,)), ((), ())),
                               preferred_element_type=F32)

    def chunk_body(n, _):
        rows = pl.ds(pl.multiple_of(n * c, c), c)
        r = r_ref[rows, :]
        k = k_ref[rows, :]
        v = v_ref[rows, :]
        th = l_ref[rows, 0:LANES]
        al = l_ref[rows, LANES:2 * LANES]
        sg = l_ref[rows, 2 * LANES:]
        z = w0 + jnp.dot(th, wd, precision=HI, preferred_element_type=F32)
        w = jnp.minimum(z, 0.0) - jnp.log(1.0 + jnp.exp(-jnp.abs(z))) - 0.5
        ld = -jnp.exp(w)
        a = _sigmoid(a0 + jnp.dot(al, wa, precision=HI, preferred_element_type=F32))
        g = jnp.dot(sg, wg, precision=HI, preferred_element_type=F32)
        kk = k * k_k
        kk = kk / jnp.maximum(jnp.sqrt(head_sum(kk * kk)), 1e-12)
        kp = k * (1.0 + (a - 1.0) * k_a)
        av = -kk
        bv = kk * a
        bonus = head_sum(r * kp * r_k) * v
        lc = jnp.dot(tri, ld, precision=HI, preferred_element_type=F32)
        ltot = lc[c - 1:c, :]
        e_l = jnp.exp(lc)
        e_nl = jnp.exp(-lc)
        rt = r * e_l
        kt = kp * e_nl
        bt = bv * e_nl
        at = av * jnp.exp(lc - ld)
        to_end = jnp.exp(ltot - lc)
        kh = kp * to_end
        bh = bv * to_end

        vb = v.astype(BF16)
        btb = bt.astype(BF16)
        ktb = kt.astype(BF16)
        per_head = []
        for h in range(2):
            mh = head0 if h == 0 else jnp.logical_not(head0)
            lhs = jnp.concatenate([jnp.where(mh, at, 0.0), jnp.where(mh, rt, 0.0)],
                                  axis=0).astype(BF16)
            xb = lax.dot_general(lhs, btb, (((1,), (1,)), ((), ())), preferred_element_type=F32)
            xk = lax.dot_general(lhs, ktb, (((1,), (1,)), ((), ())), preferred_element_type=F32)
            a_ab = jnp.where(strict, xb[:c], 0.0)
            a_rb = jnp.where(incl, xb[c:], 0.0)
            a_ak = jnp.where(strict, xk[:c], 0.0)
            a_rk = jnp.where(incl, xk[c:], 0.0)
            t_inv = _unit_lower_inverse(a_ab, HI)
            rhs = jnp.concatenate([at, bdot(a_ak, vb)], axis=1)
            y = jnp.dot(t_inv, rhs, precision=HI, preferred_element_type=F32)
            per_head.append((a_rb, a_rk, y[:, :LANES], y[:, LANES:]))
        a_st = jnp.where(head0, per_head[0][2], per_head[1][2])
        u_v = jnp.where(head0, per_head[0][3], per_head[1][3])
        au = jnp.concatenate([a_st, u_v], axis=1)
        zs = []
        for h in range(2):
            a_rb, a_rk = per_head[h][0], per_head[h][1]
            zz = bdot(a_rb, au)
            zs.append((zz[:, :LANES], zz[:, LANES:] + bdot(a_rk, vb)))
        r_st = rt + jnp.where(head0, zs[0][0], zs[1][0])
        o_v = jnp.where(head0, zs[0][1], zs[1][1])

        st = st_scr[...]
        o = jnp.dot(r_st, st, precision=HI, preferred_element_type=F32) + o_v
        trans = jnp.where(same_head, tdot(bh, a_st), 0.0) + jnp.where(eye2, jnp.exp(ltot), 0.0)
        inject = jnp.where(same_head,
                           tdot(jnp.concatenate([bh, kh], axis=0),
                                jnp.concatenate([u_v, v], axis=0)), 0.0)
        st_scr[...] = jnp.dot(trans, st, precision=HI, preferred_element_type=F32) + inject

        mean = head_sum(o) * (1.0 / HEAD_DIM)
        dev = o - mean
        var = head_sum(dev * dev) * (1.0 / HEAD_DIM)
        on = dev * lax.rsqrt(var + GN_EPS) * gn_g + gn_b
        o_ref[rows, :] = ((on + bonus) * g).astype(o_ref.dtype)
        return 0

    lax.fori_loop(0, n_chunks, chunk_body, 0)


def _rwkv(p3, l3, wd, wa, wg, cp, *, n_pairs, tt):
    b, s, _ = p3.shape
    col0 = 3 * n_pairs

    def pspec(off):
        return pl.BlockSpec((None, tt, LANES), lambda bi, hp, ti: (bi, ti, col0 + off + hp))

    wspec = lambda rows: pl.BlockSpec((rows, LANES), lambda bi, hp, ti: (0, hp))
    kern = functools.partial(_rwkv_kernel, n_chunks=tt // CHUNK)
    return pl.pallas_call(
        kern,
        grid=(b, n_pairs, s // tt),
        in_specs=[pspec(0), pspec(n_pairs), pspec(2 * n_pairs),
                  pl.BlockSpec((None, tt, LORA_PAD), lambda bi, hp, ti: (bi, ti, 0)),
                  wspec(LANES), wspec(LANES), wspec(2 * LANES), wspec(8)],
        out_specs=pl.BlockSpec((None, tt, LANES), lambda bi, hp, ti: (bi, ti, hp)),
        out_shape=jax.ShapeDtypeStruct((b, s, n_pairs * LANES), BF16),
        scratch_shapes=[pltpu.VMEM((LANES, LANES), F32)],
        compiler_params=_cparams(("arbitrary", "arbitrary", "arbitrary")),
        name="rwkv",
    )(p3, p3, p3, l3, wd, wa, wg, cp)


def _outproj_kernel(a_ref, r_ref, wa_ref, wr_ref, x_ref, gt_ref, o_ref):
    mix = jnp.dot(a_ref[...], wa_ref[...], preferred_element_type=F32)
    mix += jnp.dot(r_ref[...], wr_ref[...], preferred_element_type=F32)
    o_ref[...] = x_ref[...] + gt_ref[...] * mix


def _outproj(attn, rw, w_a, w_r, x2, gt1, *, seq, tm):
    m, d = x2.shape
    half = attn.shape[1]
    tn = d // 2
    tiles_per_batch = seq // tm
    return pl.pallas_call(
        _outproj_kernel,
        grid=(m // tm, d // tn),
        in_specs=[
            pl.BlockSpec((tm, half), lambda i, j: (i, 0)),
            pl.BlockSpec((tm, half), lambda i, j: (i, 0)),
            pl.BlockSpec((half, tn), lambda i, j: (0, j)),
            pl.BlockSpec((half, tn), lambda i, j: (0, j)),
            pl.BlockSpec((tm, tn), lambda i, j: (i, j)),
            pl.BlockSpec((None, 1, tn), lambda i, j: (i // tiles_per_batch, 0, j)),
        ],
        out_specs=pl.BlockSpec((tm, tn), lambda i, j: (i, j)),
        out_shape=jax.ShapeDtypeStruct((m, d), F32),
        compiler_params=_cparams(("arbitrary", "arbitrary")),
        name="outproj",
    )(attn, rw, w_a, w_r, x2, gt1)


def _ffn_kernel(x_ref, sc_ref, sh_ref, gt_ref, g2_ref, gf_ref, wg_ref, wu_ref, wd_ref, o_ref,
                h_scr, acc_scr):
    f = pl.program_id(1)

    def rms(x, g):
        return x * lax.rsqrt(jnp.mean(x * x, axis=-1, keepdims=True) + RMS_EPS) * g

    @pl.when(f == 0)
    def _():
        h = rms(x_ref[...], g2_ref[...]) * (1.0 + sc_ref[...]) + sh_ref[...]
        h_scr[...] = h.astype(BF16)
        acc_scr[...] = jnp.zeros_like(acc_scr)

    h = h_scr[...]
    gate = jnp.dot(h, wg_ref[...], preferred_element_type=F32)
    up = jnp.dot(h, wu_ref[...], preferred_element_type=F32)
    act = (gate * _sigmoid(gate) * up).astype(BF16)
    acc_scr[...] += jnp.dot(act, wd_ref[...], preferred_element_type=F32)

    @pl.when(f == pl.num_programs(1) - 1)
    def _():
        y = x_ref[...] + gt_ref[...] * acc_scr[...]
        o_ref[...] = rms(y, gf_ref[...])


def _ffn(x1, sc2, sh2, gt2, g2, gf, w_gate, w_up, w_down, *, seq, tm, tf):
    m, d = x1.shape
    fdim = w_gate.shape[1]
    tiles_per_batch = seq // tm
    vec = pl.BlockSpec((None, 1, d), lambda i, f: (i // tiles_per_batch, 0, 0))
    par = pl.BlockSpec((1, d), lambda i, f: (0, 0))
    return pl.pallas_call(
        _ffn_kernel,
        grid=(m // tm, fdim // tf),
        in_specs=[
            pl.BlockSpec((tm, d), lambda i, f: (i, 0)),
            vec, vec, vec, par, par,
            pl.BlockSpec((d, tf), lambda i, f: (0, f)),
            pl.BlockSpec((d, tf), lambda i, f: (0, f)),
            pl.BlockSpec((tf, d), lambda i, f: (f, 0)),
        ],
        out_specs=pl.BlockSpec((tm, d), lambda i, f: (i, 0)),
        out_shape=jax.ShapeDtypeStruct((m, d), F32),
        scratch_shapes=[pltpu.VMEM((tm, d), BF16), pltpu.VMEM((tm, d), F32)],
        compiler_params=_cparams(("arbitrary", "arbitrary")),
        name="ffn",
    )(x1, sc2, sh2, gt2, g2, gf, w_gate, w_up, w_down)


def _largest_tile(total, cap, step):
    t = min(total, cap)
    t -= t % step
    while total % t:
        t -= step
    return t


def kernel(x, c, positions, w_ada, b_ada, norm1_g, norm2_g, normf_g, w_in, w_out, mu_shift, w0,
           w_decay_up, a0, w_iclr_up, w_gate_up, k_k, k_a, r_k, gn_g, gn_b, w_ffn_gate,
           w_ffn_up, w_ffn_down):
    b, s, d = x.shape
    depth = w_ada.shape[0]
    assert depth == 1, "the final norm is fused into the (single) channel mixer"
    half = d // 2
    n_pairs = half // LANES
    m = b * s
    assert s % SPAN == 0 and half % LANES == 0
    tm = _largest_tile(s, 512, 8)
    tt = _largest_tile(s, 512, CHUNK)
    fdim = w_ffn_gate.shape[-1]
    tf = _largest_tile(fdim, 512, LANES)

    posb = jnp.broadcast_to(positions.astype(F32).reshape(m, 1), (m, LANES))
    x2 = x.reshape(m, d)
    c8 = jnp.pad(c, ((0, 8 - b % 8 if b % 8 else 0), (0, 0)))

    def lora_cols(t, axis):
        parts = jnp.split(t, [DECAY_LORA, DECAY_LORA + ICLR_LORA], axis=axis)
        offs = (0, LANES, 2 * LANES)
        shape = list(t.shape)
        shape[axis] = LORA_PAD
        out = jnp.zeros(shape, t.dtype)
        for part, off in zip(parts, offs):
            idx = [slice(None)] * t.ndim
            idx[axis] = slice(off, off + part.shape[axis])
            out = out.at[tuple(idx)].set(part)
        return out

    def pad_rows(t, rows):
        return jnp.pad(t, ((0, rows - t.shape[0]), (0, 0)))

    for i in range(depth):
        ada = _ada(c8, w_ada[i], b_ada[i][None, :])[:b]
        sh1, sc1, gt1, sh2, sc2, gt2 = (t[:, None, :] for t in jnp.split(ada, 6, axis=-1))

        w_main = w_in[i][:, :6 * half].astype(BF16)
        w_lora = lora_cols(w_in[i][:, 6 * half:], 1).astype(BF16)
        mu_main = mu_shift[i][None, :3 * half]
        mu_lora = lora_cols(mu_shift[i][None, 3 * half:], 1)
        p2, l2 = _inproj(x2, posb, sc1, sh1, norm1_g[i][None, :], w_main, w_lora,
                         mu_main, mu_lora, seq=s, tm=tm)
        p3 = p2.reshape(b, s, 6 * half)
        l3 = l2.reshape(b, s, LORA_PAD)

        attn = _attention(p3, n_pairs=n_pairs)

        cp = jnp.stack([w0[i], a0[i], k_k[i], k_a[i], r_k[i].reshape(-1), gn_g[i], gn_b[i],
                        jnp.zeros_like(w0[i])])
        rw = _rwkv(p3, l3, pad_rows(w_decay_up[i], LANES), pad_rows(w_iclr_up[i], LANES),
                   pad_rows(w_gate_up[i], 2 * LANES), cp, n_pairs=n_pairs, tt=tt)

        wo = w_out[i].astype(BF16)
        x1 = _outproj(attn.reshape(m, half), rw.reshape(m, half), wo[:half], wo[half:], x2, gt1,
                      seq=s, tm=tm)
        x2 = _ffn(x1, sc2, sh2, gt2, norm2_g[i][None, :], normf_g[None, :],
                  w_ffn_gate[i].astype(BF16),
                  w_ffn_up[i].astype(BF16), w_ffn_down[i].astype(BF16), seq=s, tm=tm, tf=tf)
    return x2.reshape(b, s, d)
```

```python
import functools
import math

import jax
import jax.numpy as jnp
from jax import lax
from jax.experimental import pallas as pl
from jax.experimental.pallas import tpu as pltpu

HEAD_DIM = 64
LANES = 128
ROT_DIM = HEAD_DIM // 4
ROPE_THETA = 500000.0
RMS_EPS = 1e-6
GN_EPS = HEAD_DIM * 1e-5
DILATIONS = (1, 4, 16)
WINDOW_BLOCK = 128
SPAN = WINDOW_BLOCK * max(DILATIONS)
DECAY_LORA = 64
ICLR_LORA = 64
GATE_LORA = 160
LORA_PAD = 512
CHUNK = 64
RWKV_GROUP = 8
NEG = -1e30
VMEM_LIMIT = 56 * 1024 * 1024

F32 = jnp.float32
BF16 = jnp.bfloat16
HI = lax.Precision.HIGHEST


def _cparams(sem):
    return pltpu.CompilerParams(dimension_semantics=sem, vmem_limit_bytes=VMEM_LIMIT)


def _sigmoid(z):
    return 1.0 / (1.0 + jnp.exp(-z))


def _ada_kernel(c_ref, w_ref, b_ref, o_ref):
    c = c_ref[...]
    s = c * _sigmoid(c)
    o_ref[...] = jnp.dot(s, w_ref[...], precision=HI, preferred_element_type=F32) + b_ref[...]


def _ada(c8, w_ada, b_ada):
    rows, d = c8.shape
    n = w_ada.shape[1]
    tn = _largest_tile(n, 1024, LANES)
    return pl.pallas_call(
        _ada_kernel,
        grid=(n // tn,),
        in_specs=[
            pl.BlockSpec((rows, d), lambda j: (0, 0)),
            pl.BlockSpec((d, tn), lambda j: (0, j)),
            pl.BlockSpec((1, tn), lambda j: (0, j)),
        ],
        out_specs=pl.BlockSpec((rows, tn), lambda j: (0, j)),
        out_shape=jax.ShapeDtypeStruct((rows, n), F32),
        compiler_params=_cparams(("arbitrary",)),
        name="ada",
    )(c8, w_ada, b_ada)


def _inproj_kernel(x_ref, pos_ref, sc_ref, sh_ref, g_ref, w_ref, wl_ref, mu_ref, mul_ref,
                   p_ref, l_ref, h_scr, cos_scr, sa_scr, sb_scr, last_scr, lastl_scr,
                   *, tiles_per_batch, n_attn_tiles):
    i = pl.program_id(0)
    j = pl.program_id(1)
    tm, tn = p_ref.shape
    first_in_batch = (i % tiles_per_batch) == 0

    @pl.when(j == 0)
    def _():
        x = x_ref[...]
        ms = jnp.mean(x * x, axis=-1, keepdims=True)
        y = x * lax.rsqrt(ms + RMS_EPS) * g_ref[...]
        h = y * (1.0 + sc_ref[...]) + sh_ref[...]
        h_scr[...] = h.astype(BF16)
        lane = lax.broadcasted_iota(jnp.int32, (1, LANES), 1)
        freq = (lane % (ROT_DIM // 2)).astype(F32)
        inv = jnp.exp(freq * (-2.0 / ROT_DIM * math.log(ROPE_THETA)))
        ang = pos_ref[...] * inv
        cs = jnp.cos(ang)
        sn = jnp.sin(ang)
        lm = lane % HEAD_DIM
        cos_scr[...] = jnp.where(lm < ROT_DIM, cs, 1.0)
        sa_scr[...] = jnp.where(lm < ROT_DIM // 2, -sn, 0.0)
        sb_scr[...] = jnp.where((lm >= ROT_DIM // 2) & (lm < ROT_DIM), sn, 0.0)

    res = jnp.dot(h_scr[...], w_ref[...], preferred_element_type=F32)

    def shifted(val, carry_row, mu):
        prev = pltpu.roll(val, 1, 0)
        row = lax.broadcasted_iota(jnp.int32, val.shape, 0)
        carry = jnp.where(first_in_batch, 0.0, carry_row)
        prev = jnp.where(row == 0, carry, prev)
        return val + (prev - val) * mu

    @pl.when(j < 2)
    def _():
        ct = cos_scr[...]
        sa = sa_scr[...]
        sb = sb_scr[...]
        for gidx in range(tn // LANES):
            cols = slice(gidx * LANES, (gidx + 1) * LANES)
            t = res[:, cols]
            up = pltpu.roll(t, LANES - ROT_DIM // 2, 1)
            dn = pltpu.roll(t, ROT_DIM // 2, 1)
            p_ref[:, cols] = t * ct + up * sa + dn * sb

    @pl.when(j == 2)
    def _():
        p_ref[...] = res

    @pl.when(j >= n_attn_tiles)
    def _():
        jj = j - n_attn_tiles
        carry_row = last_scr[jj, 7:8, :]
        p_ref[...] = shifted(res, carry_row, mu_ref[...])
        last_scr[jj] = res[tm - 8:, :]

    @pl.when(j == n_attn_tiles)
    def _():
        resl = jnp.dot(h_scr[...], wl_ref[...], preferred_element_type=F32)
        y = shifted(resl, lastl_scr[7:8, :], mul_ref[...])
        lastl_scr[...] = resl[tm - 8:, :]
        l_ref[:, 0:LANES] = jnp.tanh(y[:, 0:LANES])
        l_ref[:, LANES:2 * LANES] = y[:, LANES:2 * LANES]
        l_ref[:, 2 * LANES:] = _sigmoid(y[:, 2 * LANES:])


def _inproj(x2, posb, sc1, sh1, g1, w_main, w_lora, mu_main, mu_lora, *, seq, tm):
    m, d = x2.shape
    n = w_main.shape[1]
    tn = n // 6
    n_attn_tiles = 3
    tiles_per_batch = seq // tm
    kern = functools.partial(_inproj_kernel, tiles_per_batch=tiles_per_batch,
                             n_attn_tiles=n_attn_tiles)
    return pl.pallas_call(
        kern,
        grid=(m // tm, 6),
        in_specs=[
            pl.BlockSpec((tm, d), lambda i, j: (i, 0)),
            pl.BlockSpec((tm, LANES), lambda i, j: (i, 0)),
            pl.BlockSpec((None, 1, d), lambda i, j: (i // tiles_per_batch, 0, 0)),
            pl.BlockSpec((None, 1, d), lambda i, j: (i // tiles_per_batch, 0, 0)),
            pl.BlockSpec((1, d), lambda i, j: (0, 0)),
            pl.BlockSpec((d, tn), lambda i, j: (0, j)),
            pl.BlockSpec((d, LORA_PAD), lambda i, j: (0, 0)),
            pl.BlockSpec((1, tn), lambda i, j: (0, jnp.maximum(j - n_attn_tiles, 0))),
            pl.BlockSpec((1, LORA_PAD), lambda i, j: (0, 0)),
        ],
        out_specs=[
            pl.BlockSpec((tm, tn), lambda i, j: (i, j)),
            pl.BlockSpec((tm, LORA_PAD), lambda i, j: (i, 0)),
        ],
        out_shape=[
            jax.ShapeDtypeStruct((m, n), F32),
            jax.ShapeDtypeStruct((m, LORA_PAD), F32),
        ],
        scratch_shapes=[
            pltpu.VMEM((tm, d), BF16),
            pltpu.VMEM((tm, LANES), F32),
            pltpu.VMEM((tm, LANES), F32),
            pltpu.VMEM((tm, LANES), F32),
            pltpu.VMEM((3, 8, tn), F32),
            pltpu.VMEM((8, LORA_PAD), F32),
        ],
        compiler_params=_cparams(("arbitrary", "arbitrary")),
        name="inproj",
    )(x2, posb, sc1, sh1, g1, w_main, w_lora, mu_main, mu_lora)


def _attn_kernel(q_ref, kc_ref, kp_ref, vc_ref, vp_ref, o_ref, pv_scr, m_scr, l_scr):
    blk = WINDOW_BLOCK
    has_prev = pl.program_id(2) > 0
    lane = lax.broadcasted_iota(jnp.int32, (1, LANES), 1)
    head0 = lane < HEAD_DIM
    qi = lax.broadcasted_iota(jnp.int32, (blk, 2 * blk), 0)
    kj = lax.broadcasted_iota(jnp.int32, (blk, 2 * blk), 1)
    in_cur = kj >= blk
    slack = jnp.where(in_cur, qi - (kj - blk), kj - qi)
    bias_full = jnp.where(slack >= 0, 0.0, NEG)
    bias_cur = jnp.where(in_cur, bias_full, NEG)
    bias_edge = jnp.where(has_prev, bias_full, bias_cur)

    def block(d, q_start, prev_ref, prev_start, bias):
        def rows(start):
            return pl.ds(start, blk, stride=d) if d > 1 else pl.ds(start, blk)
        q = q_ref[rows(q_start), :] * (HEAD_DIM ** -0.5)
        kwin = jnp.concatenate([prev_ref[0][rows(prev_start), :], kc_ref[rows(q_start), :]],
                               axis=0).astype(BF16)
        vwin = jnp.concatenate([prev_ref[1][rows(prev_start), :], vc_ref[rows(q_start), :]],
                               axis=0).astype(BF16)
        outs = []
        for h in range(2):
            mh = head0 if h == 0 else jnp.logical_not(head0)
            qh = jnp.where(mh, q, 0.0).astype(BF16)
            s = lax.dot_general(qh, kwin, (((1,), (1,)), ((), ())),
                                preferred_element_type=F32) + bias
            mx = jnp.max(s, axis=-1, keepdims=True)
            p = jnp.exp(s - mx)
            den = jnp.sum(p, axis=-1, keepdims=True)
            pv = jnp.dot(p.astype(BF16), vwin, preferred_element_type=F32)
            outs.append((pv, mx, den))
        pv = jnp.where(head0, outs[0][0], outs[1][0])
        mx = jnp.where(head0, outs[0][1], outs[1][1])
        den = jnp.where(head0, outs[0][2], outs[1][2])
        return pv, mx, den, rows(q_start)

    cur = (kc_ref, vc_ref)
    prv = (kp_ref, vp_ref)

    for slot, d in ((0, DILATIONS[1]), (1, DILATIONS[2])):
        unit = blk * d
        n_units = SPAN // unit

        def store(res, slot=slot):
            pv, mx, den, rr = res
            pv_scr[slot, rr, :] = pv
            m_scr[slot, rr, :] = mx
            l_scr[slot, rr, :] = den

        def edge_body(r, _, d=d, unit=unit, n_units=n_units, store=store):
            store(block(d, r, prv, (n_units - 1) * unit + r, bias_edge))
            return 0
        lax.fori_loop(0, d, edge_body, 0)

        if n_units > 1:
            def inner_body(t, _, d=d, unit=unit, store=store):
                n = t // d + 1
                r = t % d
                start = n * unit + r
                store(block(d, start, cur, start - unit, bias_full))
                return 0
            lax.fori_loop(0, (n_units - 1) * d, inner_body, 0)

    def finish(res):
        pv, mx, den, rr = res
        m4, m16 = m_scr[0, rr, :], m_scr[1, rr, :]
        mm = jnp.maximum(mx, jnp.maximum(m4, m16))
        e1, e4, e16 = jnp.exp(mx - mm), jnp.exp(m4 - mm), jnp.exp(m16 - mm)
        num = e1 * pv + e4 * pv_scr[0, rr, :] + e16 * pv_scr[1, rr, :]
        dn = e1 * den + e4 * l_scr[0, rr, :] + e16 * l_scr[1, rr, :]
        o_ref[rr, :] = (num / dn).astype(o_ref.dtype)

    finish(block(1, 0, prv, SPAN - blk, bias_edge))

    def dense_body(n, _):
        start = pl.multiple_of(n * blk, blk)
        finish(block(1, start, cur, start - blk, bias_full))
        return 0
    lax.fori_loop(1, SPAN // blk, dense_body, 0)


def _attention(p3, *, n_pairs):
    b, s, _ = p3.shape
    n_spans = s // SPAN

    def spec(col0, prev):
        if prev:
            return pl.BlockSpec((None, SPAN, LANES),
                                lambda bi, hp, si: (bi, jnp.maximum(si - 1, 0), col0 + hp))
        return pl.BlockSpec((None, SPAN, LANES), lambda bi, hp, si: (bi, si, col0 + hp))

    return pl.pallas_call(
        _attn_kernel,
        grid=(b, n_pairs, n_spans),
        in_specs=[spec(0, False), spec(n_pairs, False), spec(n_pairs, True),
                  spec(2 * n_pairs, False), spec(2 * n_pairs, True)],
        out_specs=pl.BlockSpec((None, SPAN, LANES), lambda bi, hp, si: (bi, si, hp)),
        out_shape=jax.ShapeDtypeStruct((b, s, n_pairs * LANES), BF16),
        scratch_shapes=[pltpu.VMEM((2, SPAN, LANES), F32)] * 3,
        compiler_params=_cparams(("arbitrary", "arbitrary", "arbitrary")),
        name="attn",
    )(p3, p3, p3, p3, p3)


def _unit_lower_inverse(a, masks):
    eye, diag4, offs = masks
    mm = lambda xs, ys: [jnp.dot(x.astype(BF16), y.astype(BF16), preferred_element_type=F32)
                         for x, y in zip(xs, ys)]
    ad = [jnp.where(diag4, x, 0.0) for x in a]
    a2 = mm(ad, ad)
    a3 = mm(a2, ad)
    t = [eye + x1 + x2 + x3 for x1, x2, x3 in zip(ad, a2, a3)]
    for off in offs:
        inner = mm([jnp.where(off, x, 0.0) for x in a], t)
        t = [x + y for x, y in zip(t, mm(t, inner))]
    return t


def _rwkv_kernel(r_ref, k_ref, v_ref, l_ref, wd_ref, wa_ref, wg_ref, cp_ref, o_ref, st_scr,
                 *, n_groups, group):
    @pl.when(pl.program_id(2) == 0)
    def _():
        st_scr[...] = jnp.zeros_like(st_scr)

    c = CHUNK
    lane = lax.broadcasted_iota(jnp.int32, (1, LANES), 1)
    head0 = lane < HEAD_DIM
    ri = lax.broadcasted_iota(jnp.int32, (2 * c, 2 * c), 0)
    ci = lax.broadcasted_iota(jnp.int32, (2 * c, 2 * c), 1)
    same = lambda size: (ri // size) == (ci // size)
    same_head = same(c)
    strict = same_head & (ci < ri)
    incl = same_head & (ci <= ri)
    eye = (ri == ci).astype(F32)
    sizes = []
    size = 4
    while size < c:
        sizes.append(size)
        size *= 2
    inv_masks = (eye, same(4), [same(2 * sz) & jnp.logical_not(same(sz)) for sz in sizes])
    tri = (lax.broadcasted_iota(jnp.int32, (c, c), 1)
           <= lax.broadcasted_iota(jnp.int32, (c, c), 0)).astype(BF16)

    cp = cp_ref[...]
    w0, a0, k_k, k_a, r_k, gn_g, gn_b = (cp[n:n + 1, :] for n in range(7))
    wd = wd_ref[...]
    wa = wa_ref[...]
    wg = wg_ref[...]

    def head_sum(x):
        s0 = jnp.sum(jnp.where(head0, x, 0.0), axis=-1, keepdims=True)
        s1 = jnp.sum(jnp.where(head0, 0.0, x), axis=-1, keepdims=True)
        return jnp.where(head0, s0, s1)

    def pick(x2):
        return jnp.where(head0, x2[:c], x2[c:])

    def bdot(x, y):
        return jnp.dot(x.astype(BF16), y.astype(BF16), preferred_element_type=F32)

    def bdot_t(x, y):
        return lax.dot_general(x.astype(BF16), y.astype(BF16), (((1,), (1,)), ((), ())),
                               preferred_element_type=F32)

    def tdot(x, y):
        return lax.dot_general(x.astype(BF16), y.astype(BF16), (((0,), (0,)), ((), ())),
                               preferred_element_type=F32)

    def split(x):
        hi = x.astype(BF16)
        return hi, (x - hi.astype(F32)).astype(BF16)

    def group_terms(rows):
        chunks = lambda x: [x[n * c:(n + 1) * c] for n in range(group)]
        each = lambda f, *xs: [f(*args) for args in zip(*xs)]
        r = r_ref[rows, :]
        k = k_ref[rows, :]
        v = v_ref[rows, :]
        th = l_ref[rows, 0:LANES]
        al = l_ref[rows, LANES:2 * LANES]
        sg = l_ref[rows, 2 * LANES:]
        z = w0 + bdot(th, wd)
        w = jnp.minimum(z, 0.0) - jnp.log(1.0 + jnp.exp(-jnp.abs(z))) - 0.5
        ld = -jnp.exp(w)
        a = _sigmoid(a0 + bdot(al, wa))
        g = bdot(sg, wg)
        kk = k * k_k
        kk = kk / jnp.maximum(jnp.sqrt(head_sum(kk * kk)), 1e-12)
        kp = k * (1.0 + (a - 1.0) * k_a)
        av = -kk
        bv = kk * a
        bonus = head_sum(r * kp * r_k) * v
        ld_hi, ld_lo = split(ld)
        lc = jnp.concatenate(
            each(lambda hi, lo: (jnp.dot(tri, hi, preferred_element_type=F32)
                                 + jnp.dot(tri, lo, preferred_element_type=F32)),
                 chunks(ld_hi), chunks(ld_lo)), axis=0)
        ltot = [x[c - 1:c, :] for x in chunks(lc)]
        ltot_rows = jnp.concatenate([jnp.broadcast_to(x, (c, LANES)) for x in ltot], axis=0)
        e_nl = jnp.exp(-lc)
        rt = r * jnp.exp(lc)
        kt = kp * e_nl
        bt = bv * e_nl
        at = av * jnp.exp(lc - ld)
        to_end = jnp.exp(ltot_rows - lc)
        kh = kp * to_end
        bh = bv * to_end
        at0, at1 = jnp.where(head0, at, 0.0), jnp.where(head0, 0.0, at)
        rt0, rt1 = jnp.where(head0, rt, 0.0), jnp.where(head0, 0.0, rt)

        aa = each(lambda a0_, a1_, r0_, r1_, b_, k_: bdot_t(
            jnp.concatenate([a0_, a1_, r0_, r1_], axis=0), jnp.concatenate([b_, b_, k_, k_], axis=0)),
            chunks(at0), chunks(at1), chunks(rt0), chunks(rt1), chunks(bt), chunks(kt))
        a_ab = [jnp.where(strict, x[:2 * c, :2 * c], 0.0) for x in aa]
        a_ak = [jnp.where(strict, x[:2 * c, 2 * c:], 0.0) for x in aa]
        a_rb = [jnp.where(incl, x[2 * c:, :2 * c], 0.0) for x in aa]
        a_rk = [jnp.where(incl, x[2 * c:, 2 * c:], 0.0) for x in aa]
        v2 = [jnp.concatenate([x, x], axis=0) for x in chunks(v)]
        av2 = each(bdot, a_ak, v2)
        t_inv = _unit_lower_inverse(a_ab, inv_masks)
        y2 = each(lambda t_, a_, av_: bdot(t_, jnp.concatenate(
            [jnp.concatenate([a_, a_], axis=0), av_], axis=1)), t_inv, chunks(at), av2)
        a_st = [pick(x[:, :LANES]) for x in y2]
        u_v = [pick(x[:, LANES:]) for x in y2]

        def readout(a_rb_, a_rk_, a_st_, u_v_, v2_):
            top = jnp.concatenate([jnp.concatenate([a_st_, a_st_], axis=0),
                                   jnp.concatenate([u_v_, u_v_], axis=0)], axis=1)
            bot = jnp.concatenate([jnp.zeros_like(v2_), v2_], axis=1)
            return bdot(jnp.concatenate([a_rb_, a_rk_], axis=1), jnp.concatenate([top, bot], axis=0))
        zz = each(readout, a_rb, a_rk, a_st, u_v, v2)
        r_st = [x + pick(y[:, :LANES]) for x, y in zip(chunks(rt), zz)]
        o_v = [pick(y[:, LANES:]) for y in zz]
        trans = each(lambda a_, b_: jnp.where(same_head, tdot(a_, b_), 0.0), a_st, chunks(bh))
        inject = each(lambda u_, v_, b_, k_: jnp.where(same_head, tdot(
            jnp.concatenate([u_, v_], axis=0), jnp.concatenate([b_, k_], axis=0)), 0.0),
            u_v, chunks(v), chunks(bh), chunks(kh))
        decay = [jnp.exp(x) for x in ltot]
        return zip(r_st, o_v, trans, inject, decay, chunks(bonus), chunks(g))

    def group_body(gi, _):
        base = pl.multiple_of(gi * (group * c), group * c)
        rows = [pl.ds(pl.multiple_of(base + n * c, c), c) for n in range(group)]
        terms = group_terms(pl.ds(base, group * c))
        st = st_scr[...]
        for rr, (r_st, o_v, trans, inject, decay, bonus, g) in zip(rows, terms):
            o = bdot_t(r_st, st) + o_v
            st_hi, st_lo = split(st)
            tb = trans.astype(BF16)
            st = (st * decay + jnp.dot(st_hi, tb, preferred_element_type=F32)
                  + jnp.dot(st_lo, tb, preferred_element_type=F32) + inject)
            mean = head_sum(o) * (1.0 / HEAD_DIM)
            dev = o - mean
            var = head_sum(dev * dev) * (1.0 / HEAD_DIM)
            on = dev * lax.rsqrt(var + GN_EPS) * gn_g + gn_b
            o_ref[rr, :] = ((on + bonus) * g).astype(o_ref.dtype)
        st_scr[...] = st
        return 0

    lax.fori_loop(0, n_groups, group_body, 0)


def _rwkv(p3, l3, wd, wa, wg, cp, *, n_pairs, tt, group):
    b, s, _ = p3.shape
    col0 = 3 * n_pairs

    def pspec(off):
        return pl.BlockSpec((None, tt, LANES), lambda bi, hp, ti: (bi, ti, col0 + off + hp))

    wspec = lambda rows: pl.BlockSpec((rows, LANES), lambda bi, hp, ti: (0, hp))
    kern = functools.partial(_rwkv_kernel, n_groups=tt // (group * CHUNK), group=group)
    return pl.pallas_call(
        kern,
        grid=(b, n_pairs, s // tt),
        in_specs=[pspec(0), pspec(n_pairs), pspec(2 * n_pairs),
                  pl.BlockSpec((None, tt, LORA_PAD), lambda bi, hp, ti: (bi, ti, 0)),
                  wspec(LANES), wspec(LANES), wspec(2 * LANES), wspec(8)],
        out_specs=pl.BlockSpec((None, tt, LANES), lambda bi, hp, ti: (bi, ti, hp)),
        out_shape=jax.ShapeDtypeStruct((b, s, n_pairs * LANES), BF16),
        scratch_shapes=[pltpu.VMEM((LANES, LANES), F32)],
        compiler_params=_cparams(("arbitrary", "arbitrary", "arbitrary")),
        name="rwkv",
    )(p3, p3, p3, l3, wd, wa, wg, cp)


def _outproj_kernel(a_ref, r_ref, wa_ref, wr_ref, x_ref, gt_ref, o_ref):
    mix = jnp.dot(a_ref[...], wa_ref[...], preferred_element_type=F32)
    mix += jnp.dot(r_ref[...], wr_ref[...], preferred_element_type=F32)
    o_ref[...] = x_ref[...] + gt_ref[...] * mix


def _outproj(attn, rw, w_a, w_r, x2, gt1, *, seq, tm):
    m, d = x2.shape
    half = attn.shape[1]
    tn = d // 2
    tiles_per_batch = seq // tm
    return pl.pallas_call(
        _outproj_kernel,
        grid=(m // tm, d // tn),
        in_specs=[
            pl.BlockSpec((tm, half), lambda i, j: (i, 0)),
            pl.BlockSpec((tm, half), lambda i, j: (i, 0)),
            pl.BlockSpec((half, tn), lambda i, j: (0, j)),
            pl.BlockSpec((half, tn), lambda i, j: (0, j)),
            pl.BlockSpec((tm, tn), lambda i, j: (i, j)),
            pl.BlockSpec((None, 1, tn), lambda i, j: (i // tiles_per_batch, 0, j)),
        ],
        out_specs=pl.BlockSpec((tm, tn), lambda i, j: (i, j)),
        out_shape=jax.ShapeDtypeStruct((m, d), F32),
        compiler_params=_cparams(("arbitrary", "arbitrary")),
        name="outproj",
    )(attn, rw, w_a, w_r, x2, gt1)


def _ffn_kernel(x_ref, sc_ref, sh_ref, gt_ref, g2_ref, gf_ref, wg_ref, wu_ref, wd_ref, o_ref,
                h_scr, acc_scr):
    f = pl.program_id(1)

    def rms(x, g):
        return x * lax.rsqrt(jnp.mean(x * x, axis=-1, keepdims=True) + RMS_EPS) * g

    @pl.when(f == 0)
    def _():
        h = rms(x_ref[...], g2_ref[...]) * (1.0 + sc_ref[...]) + sh_ref[...]
        h_scr[...] = h.astype(BF16)
        acc_scr[...] = jnp.zeros_like(acc_scr)

    h = h_scr[...]
    gate = jnp.dot(h, wg_ref[...], preferred_element_type=F32)
    up = jnp.dot(h, wu_ref[...], preferred_element_type=F32)
    act = (gate * _sigmoid(gate) * up).astype(BF16)
    acc_scr[...] += jnp.dot(act, wd_ref[...], preferred_element_type=F32)

    @pl.when(f == pl.num_programs(1) - 1)
    def _():
        y = x_ref[...] + gt_ref[...] * acc_scr[...]
        o_ref[...] = rms(y, gf_ref[...])


def _ffn(x1, sc2, sh2, gt2, g2, gf, w_gate, w_up, w_down, *, seq, tm, tf):
    m, d = x1.shape
    fdim = w_gate.shape[1]
    tiles_per_batch = seq // tm
    vec = pl.BlockSpec((None, 1, d), lambda i, f: (i // tiles_per_batch, 0, 0))
    par = pl.BlockSpec((1, d), lambda i, f: (0, 0))
    return pl.pallas_call(
        _ffn_kernel,
        grid=(m // tm, fdim // tf),
        in_specs=[
            pl.BlockSpec((tm, d), lambda i, f: (i, 0)),
            vec, vec, vec, par, par,
            pl.BlockSpec((d, tf), lambda i, f: (0, f)),
            pl.BlockSpec((d, tf), lambda i, f: (0, f)),
            pl.BlockSpec((tf, d), lambda i, f: (f, 0)),
        ],
        out_specs=pl.BlockSpec((tm, d), lambda i, f: (i, 0)),
        out_shape=jax.ShapeDtypeStruct((m, d), F32),
        scratch_shapes=[pltpu.VMEM((tm, d), BF16), pltpu.VMEM((tm, d), F32)],
        compiler_params=_cparams(("arbitrary", "arbitrary")),
        name="ffn",
    )(x1, sc2, sh2, gt2, g2, gf, w_gate, w_up, w_down)


def _largest_tile(total, cap, step):
    t = min(total, cap)
    t -= t % step
    while total % t:
        t -= step
    return t


def kernel(x, c, positions, w_ada, b_ada, norm1_g, norm2_g, normf_g, w_in, w_out, mu_shift, w0,
           w_decay_up, a0, w_iclr_up, w_gate_up, k_k, k_a, r_k, gn_g, gn_b, w_ffn_gate,
           w_ffn_up, w_ffn_down):
    b, s, d = x.shape
    depth = w_ada.shape[0]
    assert depth == 1, "the final norm is fused into the (single) channel mixer"
    half = d // 2
    n_pairs = half // LANES
    m = b * s
    assert s % SPAN == 0 and half % LANES == 0
    tm = _largest_tile(s, 512, 8)
    tt = _largest_tile(s, 512, RWKV_GROUP * CHUNK)
    fdim = w_ffn_gate.shape[-1]
    tf = _largest_tile(fdim, 512, LANES)

    posb = jnp.broadcast_to(positions.astype(F32).reshape(m, 1), (m, LANES))
    x2 = x.reshape(m, d)
    c8 = jnp.pad(c, ((0, 8 - b % 8 if b % 8 else 0), (0, 0)))

    def lora_cols(t, axis):
        parts = jnp.split(t, [DECAY_LORA, DECAY_LORA + ICLR_LORA], axis=axis)
        offs = (0, LANES, 2 * LANES)
        shape = list(t.shape)
        shape[axis] = LORA_PAD
        out = jnp.zeros(shape, t.dtype)
        for part, off in zip(parts, offs):
            idx = [slice(None)] * t.ndim
            idx[axis] = slice(off, off + part.shape[axis])
            out = out.at[tuple(idx)].set(part)
        return out

    def pad_rows(t, rows):
        return jnp.pad(t, ((0, rows - t.shape[0]), (0, 0)))

    for i in range(depth):
        ada = _ada(c8, w_ada[i], b_ada[i][None, :])[:b]
        sh1, sc1, gt1, sh2, sc2, gt2 = (t[:, None, :] for t in jnp.split(ada, 6, axis=-1))

        w_main = w_in[i][:, :6 * half].astype(BF16)
        w_lora = lora_cols(w_in[i][:, 6 * half:], 1).astype(BF16)
        mu_main = mu_shift[i][None, :3 * half]
        mu_lora = lora_cols(mu_shift[i][None, 3 * half:], 1)
        p2, l2 = _inproj(x2, posb, sc1, sh1, norm1_g[i][None, :], w_main, w_lora,
                         mu_main, mu_lora, seq=s, tm=tm)
        p3 = p2.reshape(b, s, 6 * half)
        l3 = l2.reshape(b, s, LORA_PAD)

        attn = _attention(p3, n_pairs=n_pairs)

        cp = jnp.stack([w0[i], a0[i], k_k[i], k_a[i], r_k[i].reshape(-1), gn_g[i], gn_b[i],
                        jnp.zeros_like(w0[i])])
        rw = _rwkv(p3, l3, pad_rows(w_decay_up[i], LANES), pad_rows(w_iclr_up[i], LANES),
                   pad_rows(w_gate_up[i], 2 * LANES), cp, n_pairs=n_pairs, tt=tt,
                   group=RWKV_GROUP)

        wo = w_out[i].astype(BF16)
        x1 = _outproj(attn.reshape(m, half), rw.reshape(m, half), wo[:half], wo[half:], x2, gt1,
                      seq=s, tm=tm)
        x2 = _ffn(x1, sc2, sh2, gt2, norm2_g[i][None, :], normf_g[None, :],
                  w_ffn_gate[i].astype(BF16),
                  w_ffn_up[i].astype(BF16), w_ffn_down[i].astype(BF16), seq=s, tm=tm, tf=tf)
    return x2.reshape(b, s, d)
```

```python
import functools
import math

import jax
import jax.numpy as jnp
from jax import lax
from jax.experimental import pallas as pl
from jax.experimental.pallas import tpu as pltpu

HEAD_DIM = 64
LANES = 128
ROT_DIM = HEAD_DIM // 4
ROPE_THETA = 500000.0
RMS_EPS = 1e-6
GN_EPS = HEAD_DIM * 1e-5
DILATIONS = (1, 4, 16)
WINDOW_BLOCK = 128
SPAN = WINDOW_BLOCK * max(DILATIONS)
DECAY_LORA = 64
ICLR_LORA = 64
GATE_LORA = 160
LORA_PAD = 512
CHUNK = 64
ATTN_GROUP = 4
RWKV_GROUP = 8
NEG = -1e30
VMEM_LIMIT = 56 * 1024 * 1024

F32 = jnp.float32
BF16 = jnp.bfloat16
HI = lax.Precision.HIGHEST


def _cparams(sem):
    return pltpu.CompilerParams(dimension_semantics=sem, vmem_limit_bytes=VMEM_LIMIT)


def _sigmoid(z):
    return 1.0 / (1.0 + jnp.exp(-z))


def _ada_kernel(c_ref, w_ref, b_ref, o_ref):
    c = c_ref[...]
    s = c * _sigmoid(c)
    o_ref[...] = jnp.dot(s, w_ref[...], precision=HI, preferred_element_type=F32) + b_ref[...]


def _ada(c8, w_ada, b_ada):
    rows, d = c8.shape
    n = w_ada.shape[1]
    tn = _largest_tile(n, 1024, LANES)
    return pl.pallas_call(
        _ada_kernel,
        grid=(n // tn,),
        in_specs=[
            pl.BlockSpec((rows, d), lambda j: (0, 0)),
            pl.BlockSpec((d, tn), lambda j: (0, j)),
            pl.BlockSpec((1, tn), lambda j: (0, j)),
        ],
        out_specs=pl.BlockSpec((rows, tn), lambda j: (0, j)),
        out_shape=jax.ShapeDtypeStruct((rows, n), F32),
        compiler_params=_cparams(("arbitrary",)),
        name="ada",
    )(c8, w_ada, b_ada)


def _inproj_kernel(x_ref, pos_ref, sc_ref, sh_ref, g_ref, w_ref, wl_ref, mu_ref, mul_ref,
                   p_ref, l_ref, h_scr, cos_scr, sa_scr, sb_scr, last_scr, lastl_scr,
                   *, tiles_per_batch, n_attn_tiles):
    i = pl.program_id(0)
    j = pl.program_id(1)
    tm, tn = p_ref.shape
    first_in_batch = (i % tiles_per_batch) == 0

    @pl.when(j == 0)
    def _():
        x = x_ref[...]
        ms = jnp.mean(x * x, axis=-1, keepdims=True)
        y = x * lax.rsqrt(ms + RMS_EPS) * g_ref[...]
        h = y * (1.0 + sc_ref[...]) + sh_ref[...]
        h_scr[...] = h.astype(BF16)
        lane = lax.broadcasted_iota(jnp.int32, (1, LANES), 1)
        freq = (lane % (ROT_DIM // 2)).astype(F32)
        inv = jnp.exp(freq * (-2.0 / ROT_DIM * math.log(ROPE_THETA)))
        ang = pos_ref[...] * inv
        cs = jnp.cos(ang)
        sn = jnp.sin(ang)
        lm = lane % HEAD_DIM
        cos_scr[...] = jnp.where(lm < ROT_DIM, cs, 1.0)
        sa_scr[...] = jnp.where(lm < ROT_DIM // 2, -sn, 0.0)
        sb_scr[...] = jnp.where((lm >= ROT_DIM // 2) & (lm < ROT_DIM), sn, 0.0)

    res = jnp.dot(h_scr[...], w_ref[...], preferred_element_type=F32)

    def shifted(val, carry_row, mu):
        prev = pltpu.roll(val, 1, 0)
        row = lax.broadcasted_iota(jnp.int32, val.shape, 0)
        carry = jnp.where(first_in_batch, 0.0, carry_row)
        prev = jnp.where(row == 0, carry, prev)
        return val + (prev - val) * mu

    @pl.when(j < 2)
    def _():
        ct = cos_scr[...]
        sa = sa_scr[...]
        sb = sb_scr[...]
        for gidx in range(tn // LANES):
            cols = slice(gidx * LANES, (gidx + 1) * LANES)
            t = res[:, cols]
            up = pltpu.roll(t, LANES - ROT_DIM // 2, 1)
            dn = pltpu.roll(t, ROT_DIM // 2, 1)
            p_ref[:, cols] = t * ct + up * sa + dn * sb

    @pl.when(j == 2)
    def _():
        p_ref[...] = res

    @pl.when(j >= n_attn_tiles)
    def _():
        jj = j - n_attn_tiles
        carry_row = last_scr[jj, 7:8, :]
        p_ref[...] = shifted(res, carry_row, mu_ref[...])
        last_scr[jj] = res[tm - 8:, :]

    @pl.when(j == n_attn_tiles)
    def _():
        resl = jnp.dot(h_scr[...], wl_ref[...], preferred_element_type=F32)
        y = shifted(resl, lastl_scr[7:8, :], mul_ref[...])
        lastl_scr[...] = resl[tm - 8:, :]
        l_ref[:, 0:LANES] = jnp.tanh(y[:, 0:LANES])
        l_ref[:, LANES:2 * LANES] = y[:, LANES:2 * LANES]
        l_ref[:, 2 * LANES:] = _sigmoid(y[:, 2 * LANES:])


def _inproj(x2, posb, sc1, sh1, g1, w_main, w_lora, mu_main, mu_lora, *, seq, tm):
    m, d = x2.shape
    n = w_main.shape[1]
    tn = n // 6
    n_attn_tiles = 3
    tiles_per_batch = seq // tm
    kern = functools.partial(_inproj_kernel, tiles_per_batch=tiles_per_batch,
                             n_attn_tiles=n_attn_tiles)
    return pl.pallas_call(
        kern,
        grid=(m // tm, 6),
        in_specs=[
            pl.BlockSpec((tm, d), lambda i, j: (i, 0)),
            pl.BlockSpec((tm, LANES), lambda i, j: (i, 0)),
            pl.BlockSpec((None, 1, d), lambda i, j: (i // tiles_per_batch, 0, 0)),
            pl.BlockSpec((None, 1, d), lambda i, j: (i // tiles_per_batch, 0, 0)),
            pl.BlockSpec((1, d), lambda i, j: (0, 0)),
            pl.BlockSpec((d, tn), lambda i, j: (0, j)),
            pl.BlockSpec((d, LORA_PAD), lambda i, j: (0, 0)),
            pl.BlockSpec((1, tn), lambda i, j: (0, jnp.maximum(j - n_attn_tiles, 0))),
            pl.BlockSpec((1, LORA_PAD), lambda i, j: (0, 0)),
        ],
        out_specs=[
            pl.BlockSpec((tm, tn), lambda i, j: (i, j)),
            pl.BlockSpec((tm, LORA_PAD), lambda i, j: (i, 0)),
        ],
        out_shape=[
            jax.ShapeDtypeStruct((m, n), F32),
            jax.ShapeDtypeStruct((m, LORA_PAD), F32),
        ],
        scratch_shapes=[
            pltpu.VMEM((tm, d), BF16),
            pltpu.VMEM((tm, LANES), F32),
            pltpu.VMEM((tm, LANES), F32),
            pltpu.VMEM((tm, LANES), F32),
            pltpu.VMEM((3, 8, tn), F32),
            pltpu.VMEM((8, LORA_PAD), F32),
        ],
        compiler_params=_cparams(("arbitrary", "arbitrary")),
        name="inproj",
    )(x2, posb, sc1, sh1, g1, w_main, w_lora, mu_main, mu_lora)


def _attn_kernel(q_ref, kc_ref, kp_ref, vc_ref, vp_ref, o_ref, pv_scr, m_scr, l_scr):
    blk = WINDOW_BLOCK
    has_prev = pl.program_id(2) > 0
    lane = lax.broadcasted_iota(jnp.int32, (1, LANES), 1)
    head0 = lane < HEAD_DIM
    qi = lax.broadcasted_iota(jnp.int32, (blk, 2 * blk), 0)
    kj = lax.broadcasted_iota(jnp.int32, (blk, 2 * blk), 1)
    in_cur = kj >= blk
    slack = jnp.where(in_cur, qi - (kj - blk), kj - qi)
    bias_full = jnp.where(slack >= 0, 0.0, NEG)
    bias_cur = jnp.where(in_cur, bias_full, NEG)
    bias_edge = jnp.where(has_prev, bias_full, bias_cur)
    cur = (kc_ref, vc_ref)
    prv = (kp_ref, vp_ref)

    def run_blocks(d, specs, emit):
        rows = lambda start: pl.ds(start, blk, stride=d) if d > 1 else pl.ds(start, blk)
        qs = [q_ref[rows(q0), :] * (HEAD_DIM ** -0.5) for q0, _, _, _ in specs]
        kw = [jnp.concatenate([pr[0][rows(p0), :], kc_ref[rows(q0), :]], axis=0).astype(BF16)
              for q0, pr, p0, _ in specs]
        vw = [jnp.concatenate([pr[1][rows(p0), :], vc_ref[rows(q0), :]], axis=0).astype(BF16)
              for q0, pr, p0, _ in specs]
        qh = [jnp.where(mh, q, 0.0).astype(BF16) for q in qs
              for mh in (head0, jnp.logical_not(head0))]
        two = lambda xs: [x for x in xs for _ in range(2)]
        s = [lax.dot_general(q, k, (((1,), (1,)), ((), ())), preferred_element_type=F32) + b
             for q, k, b in zip(qh, two(kw), two([sp[3] for sp in specs]))]
        mx = [jnp.max(x, axis=-1, keepdims=True) for x in s]
        p = [jnp.exp(x - m_) for x, m_ in zip(s, mx)]
        den = [jnp.sum(x, axis=-1, keepdims=True) for x in p]
        pv = [jnp.dot(x.astype(BF16), v, preferred_element_type=F32) for x, v in zip(p, two(vw))]
        for n, (q0, _, _, _) in enumerate(specs):
            emit(rows(q0), jnp.where(head0, pv[2 * n], pv[2 * n + 1]),
                 jnp.where(head0, mx[2 * n], mx[2 * n + 1]),
                 jnp.where(head0, den[2 * n], den[2 * n + 1]))

    for slot, d in ((0, DILATIONS[1]), (1, DILATIONS[2])):
        unit = blk * d
        n_units = SPAN // unit

        def park(rr, pv, mx, den, slot=slot):
            pv_scr[slot, rr, :] = pv
            m_scr[slot, rr, :] = mx
            l_scr[slot, rr, :] = den

        def edge_body(t, _, d=d, unit=unit, n_units=n_units, park=park):
            run_blocks(d, [(t * ATTN_GROUP + n, prv, (n_units - 1) * unit + t * ATTN_GROUP + n,
                            bias_edge) for n in range(ATTN_GROUP)], park)
            return 0
        lax.fori_loop(0, d // ATTN_GROUP, edge_body, 0)

        if n_units > 1:
            def inner_body(t, _, d=d, unit=unit, park=park):
                first = t * ATTN_GROUP
                base = (first // d + 1) * unit + first % d
                run_blocks(d, [(base + n, cur, base + n - unit, bias_full)
                               for n in range(ATTN_GROUP)], park)
                return 0
            lax.fori_loop(0, (n_units - 1) * d // ATTN_GROUP, inner_body, 0)

    def finish(rr, pv, mx, den):
        m4, m16 = m_scr[0, rr, :], m_scr[1, rr, :]
        mm = jnp.maximum(mx, jnp.maximum(m4, m16))
        e1, e4, e16 = jnp.exp(mx - mm), jnp.exp(m4 - mm), jnp.exp(m16 - mm)
        num = e1 * pv + e4 * pv_scr[0, rr, :] + e16 * pv_scr[1, rr, :]
        dn = e1 * den + e4 * l_scr[0, rr, :] + e16 * l_scr[1, rr, :]
        o_ref[rr, :] = (num / dn).astype(o_ref.dtype)

    run_blocks(1, [(0, prv, SPAN - blk, bias_edge)]
               + [(n * blk, cur, (n - 1) * blk, bias_full) for n in range(1, ATTN_GROUP)], finish)

    def dense_body(t, _):
        base = pl.multiple_of(t * (ATTN_GROUP * blk), ATTN_GROUP * blk)
        run_blocks(1, [(base + n * blk, cur, base + (n - 1) * blk, bias_full)
                       for n in range(ATTN_GROUP)], finish)
        return 0
    lax.fori_loop(1, SPAN // (blk * ATTN_GROUP), dense_body, 0)


def _attention(p3, *, n_pairs):
    b, s, _ = p3.shape
    n_spans = s // SPAN

    def spec(col0, prev):
        if prev:
            return pl.BlockSpec((None, SPAN, LANES),
                                lambda bi, hp, si: (bi, jnp.maximum(si - 1, 0), col0 + hp))
        return pl.BlockSpec((None, SPAN, LANES), lambda bi, hp, si: (bi, si, col0 + hp))

    return pl.pallas_call(
        _attn_kernel,
        grid=(b, n_pairs, n_spans),
        in_specs=[spec(0, False), spec(n_pairs, False), spec(n_pairs, True),
                  spec(2 * n_pairs, False), spec(2 * n_pairs, True)],
        out_specs=pl.BlockSpec((None, SPAN, LANES), lambda bi, hp, si: (bi, si, hp)),
        out_shape=jax.ShapeDtypeStruct((b, s, n_pairs * LANES), BF16),
        scratch_shapes=[pltpu.VMEM((2, SPAN, LANES), F32)] * 3,
        compiler_params=_cparams(("arbitrary", "arbitrary", "arbitrary")),
        name="attn",
    )(p3, p3, p3, p3, p3)


def _unit_lower_inverse(a, masks, bd):
    eye, diag4, offs = masks
    mm = lambda xs, ys: [jnp.dot(x.astype(BF16), bd(y).astype(BF16), preferred_element_type=F32)
                         for x, y in zip(xs, ys)]
    ad = [jnp.where(diag4, x, 0.0) for x in a]
    a2 = mm(ad, ad)
    a3 = mm(a2, ad)
    t = [eye + x1 + x2 + x3 for x1, x2, x3 in zip(ad, a2, a3)]
    for off in offs:
        inner = mm([jnp.where(off, x, 0.0) for x in a], t)
        t = [x + y for x, y in zip(t, mm(t, inner))]
    return t


def _rwkv_kernel(r_ref, k_ref, v_ref, l_ref, wd_ref, wa_ref, wg_ref, cp_ref, o_ref, st_scr,
                 *, n_groups, group):
    n_streams = r_ref.shape[0]

    @pl.when(pl.program_id(1) == 0)
    def _():
        st_scr[...] = jnp.zeros_like(st_scr)

    c = CHUNK
    lane = lax.broadcasted_iota(jnp.int32, (1, LANES), 1)
    head0 = lane < HEAD_DIM
    ri = lax.broadcasted_iota(jnp.int32, (c, 2 * c), 0)
    cj = lax.broadcasted_iota(jnp.int32, (c, 2 * c), 1) % c
    same = lambda size: (ri // size) == (cj // size)
    strict = cj < ri
    incl = cj <= ri
    eye = (ri == cj).astype(F32)
    sizes = []
    size = 4
    while size < c:
        sizes.append(size)
        size *= 2
    inv_masks = (eye, same(4), [same(2 * sz) & jnp.logical_not(same(sz)) for sz in sizes])
    tri = (lax.broadcasted_iota(jnp.int32, (c, c), 1)
           <= lax.broadcasted_iota(jnp.int32, (c, c), 0)).astype(BF16)
    r2 = lax.broadcasted_iota(jnp.int32, (LANES, LANES), 0)
    c2 = lax.broadcasted_iota(jnp.int32, (LANES, LANES), 1)
    same_head = (r2 // HEAD_DIM) == (c2 // HEAD_DIM)

    cp = cp_ref[...]
    w0, a0, k_k, k_a, r_k, gn_g, gn_b = (cp[n:n + 1, :] for n in range(7))
    wd = wd_ref[...]
    wa = wa_ref[...]
    wg = wg_ref[...]

    def head_sum(x):
        s0 = jnp.sum(jnp.where(head0, x, 0.0), axis=-1, keepdims=True)
        s1 = jnp.sum(jnp.where(head0, 0.0, x), axis=-1, keepdims=True)
        return jnp.where(head0, s0, s1)

    def bd(x):
        wide = x.shape[1] // LANES
        m0 = head0 if wide == 1 else jnp.concatenate([head0] * wide, axis=1)
        return jnp.concatenate([jnp.where(m0, x, 0.0), jnp.where(m0, 0.0, x)], axis=0)

    def bdot(x, y):
        return jnp.dot(x.astype(BF16), y.astype(BF16), preferred_element_type=F32)

    def bdot_t(x, y):
        return lax.dot_general(x.astype(BF16), y.astype(BF16), (((1,), (1,)), ((), ())),
                               preferred_element_type=F32)

    def tdot(x, y):
        return lax.dot_general(x.astype(BF16), y.astype(BF16), (((0,), (0,)), ((), ())),
                               preferred_element_type=F32)

    def split(x):
        hi = x.astype(BF16)
        return hi, (x - hi.astype(F32)).astype(BF16)

    chunks = lambda x: [x[n * c:(n + 1) * c] for n in range(group)]
    each = lambda f, *xs: [f(*args) for args in zip(*xs)]

    def prepare(bi, rows):
        r = r_ref[bi, rows, :]
        k = k_ref[bi, rows, :]
        v = v_ref[bi, rows, :]
        th = l_ref[bi, rows, 0:LANES]
        al = l_ref[bi, rows, LANES:2 * LANES]
        sg = l_ref[bi, rows, 2 * LANES:]
        z = w0 + bdot(th, wd)
        w = jnp.minimum(z, 0.0) - jnp.log(1.0 + jnp.exp(-jnp.abs(z))) - 0.5
        ld = -jnp.exp(w)
        a = _sigmoid(a0 + bdot(al, wa))
        g = bdot(sg, wg)
        kk = k * k_k
        kk = kk / jnp.maximum(jnp.sqrt(head_sum(kk * kk)), 1e-12)
        kp = k * (1.0 + (a - 1.0) * k_a)
        av = -kk
        bv = kk * a
        bonus = head_sum(r * kp * r_k) * v
        ld_hi, ld_lo = split(ld)
        lc = jnp.concatenate(
            each(lambda hi, lo: (jnp.dot(tri, hi, preferred_element_type=F32)
                                 + jnp.dot(tri, lo, preferred_element_type=F32)),
                 chunks(ld_hi), chunks(ld_lo)), axis=0)
        ltot = [x[c - 1:c, :] for x in chunks(lc)]
        ltot_rows = jnp.concatenate([jnp.broadcast_to(x, (c, LANES)) for x in ltot], axis=0)
        e_nl = jnp.exp(-lc)
        to_end = jnp.exp(ltot_rows - lc)
        return dict(rt=chunks(r * jnp.exp(lc)), kt=chunks(kp * e_nl), bt=chunks(bv * e_nl),
                    at=chunks(av * jnp.exp(lc - ld)), kh=chunks(kp * to_end),
                    bh=chunks(bv * to_end), v=chunks(v), decay=[jnp.exp(x) for x in ltot],
                    bonus=chunks(bonus), g=chunks(g))

    def solve(q):
        aa = each(lambda a_, r_, b_, k_: bdot_t(jnp.concatenate([a_, r_], axis=0),
                                                jnp.concatenate([bd(b_), bd(k_)], axis=0)),
                  q["at"], q["rt"], q["bt"], q["kt"])
        a_ab = [jnp.where(strict, x[:c, :LANES], 0.0) for x in aa]
        a_ak = [jnp.where(strict, x[:c, LANES:], 0.0) for x in aa]
        a_rb = [jnp.where(incl, x[c:, :LANES], 0.0) for x in aa]
        a_rk = [jnp.where(incl, x[c:, LANES:], 0.0) for x in aa]
        av = each(lambda a_, v_: bdot(a_, bd(v_)), a_ak, q["v"])
        t_inv = _unit_lower_inverse(a_ab, inv_masks, bd)
        y = each(lambda t_, a_, av_: bdot(t_, bd(jnp.concatenate([a_, av_], axis=1))),
                 t_inv, q["at"], av)
        a_st = [x[:, :LANES] for x in y]
        u_v = [x[:, LANES:] for x in y]
        zz = each(lambda rb_, rk_, y_, v_: bdot(
            jnp.concatenate([rb_, rk_], axis=1),
            jnp.concatenate([bd(y_), bd(jnp.concatenate([jnp.zeros_like(v_), v_], axis=1))], axis=0)),
            a_rb, a_rk, y, q["v"])
        r_st = [x + z_[:, :LANES] for x, z_ in zip(q["rt"], zz)]
        o_v = [z_[:, LANES:] for z_ in zz]
        trans = each(lambda a_, b_: jnp.where(same_head, tdot(a_, b_), 0.0), a_st, q["bh"])

        def injected(u_, v_, b_, k_):
            full = tdot(jnp.concatenate([u_, v_], axis=0), jnp.concatenate([b_, k_], axis=0))
            return jnp.where(head0, full[:HEAD_DIM], full[HEAD_DIM:])
        inject = each(injected, u_v, q["v"], q["bh"], q["kh"])
        return r_st, o_v, trans, inject

    def group_body(gi, _):
        base = pl.multiple_of(gi * (group * c), group * c)
        q = {}
        for bi in range(n_streams):
            for name, vals in prepare(bi, pl.ds(base, group * c)).items():
                q.setdefault(name, []).extend(vals)
        r_st, o_v, trans, inject = solve(q)
        st = [st_scr[bi] for bi in range(n_streams)]
        for n in range(group):
            idx = [bi * group + n for bi in range(n_streams)]
            o = [bdot_t(r_st[i], bd(s)) + o_v[i] for i, s in zip(idx, st)]
            parts = [split(s) for s in st]
            tb = [trans[i].astype(BF16) for i in idx]
            hi = [jnp.dot(p_[0], t_, preferred_element_type=F32) for p_, t_ in zip(parts, tb)]
            lo = [jnp.dot(p_[1], t_, preferred_element_type=F32) for p_, t_ in zip(parts, tb)]
            st = [s * q["decay"][i] + h_ + l_ + inject[i]
                  for s, i, h_, l_ in zip(st, idx, hi, lo)]
            rows = pl.ds(pl.multiple_of(base + n * c, c), c)
            for bi, i in enumerate(idx):
                mean = head_sum(o[bi]) * (1.0 / HEAD_DIM)
                dev = o[bi] - mean
                var = head_sum(dev * dev) * (1.0 / HEAD_DIM)
                on = dev * lax.rsqrt(var + GN_EPS) * gn_g + gn_b
                o_ref[bi, rows, :] = ((on + q["bonus"][i]) * q["g"][i]).astype(o_ref.dtype)
        for bi in range(n_streams):
            st_scr[bi] = st[bi]
        return 0

    lax.fori_loop(0, n_groups, group_body, 0)


def _rwkv(p3, l3, wd, wa, wg, cp, *, n_pairs, tt, group):
    b, s, _ = p3.shape
    col0 = 3 * n_pairs

    def pspec(off):
        return pl.BlockSpec((b, tt, LANES), lambda hp, ti: (0, ti, col0 + off + hp))

    wspec = lambda rows: pl.BlockSpec((rows, LANES), lambda hp, ti: (0, hp))
    kern = functools.partial(_rwkv_kernel, n_groups=tt // (group * CHUNK), group=group)
    return pl.pallas_call(
        kern,
        grid=(n_pairs, s // tt),
        in_specs=[pspec(0), pspec(n_pairs), pspec(2 * n_pairs),
                  pl.BlockSpec((b, tt, LORA_PAD), lambda hp, ti: (0, ti, 0)),
                  wspec(LANES), wspec(LANES), wspec(2 * LANES), wspec(8)],
        out_specs=pl.BlockSpec((b, tt, LANES), lambda hp, ti: (0, ti, hp)),
        out_shape=jax.ShapeDtypeStruct((b, s, n_pairs * LANES), BF16),
        scratch_shapes=[pltpu.VMEM((b, HEAD_DIM, LANES), F32)],
        compiler_params=_cparams(("arbitrary", "arbitrary")),
        name="rwkv",
    )(p3, p3, p3, l3, wd, wa, wg, cp)


def _outproj_kernel(a_ref, r_ref, wa_ref, wr_ref, x_ref, gt_ref, o_ref):
    mix = jnp.dot(a_ref[...], wa_ref[...], preferred_element_type=F32)
    mix += jnp.dot(r_ref[...], wr_ref[...], preferred_element_type=F32)
    o_ref[...] = x_ref[...] + gt_ref[...] * mix


def _outproj(attn, rw, w_a, w_r, x2, gt1, *, seq, tm):
    m, d = x2.shape
    half = attn.shape[1]
    tn = d // 2
    tiles_per_batch = seq // tm
    return pl.pallas_call(
        _outproj_kernel,
        grid=(m // tm, d // tn),
        in_specs=[
            pl.BlockSpec((tm, half), lambda i, j: (i, 0)),
            pl.BlockSpec((tm, half), lambda i, j: (i, 0)),
            pl.BlockSpec((half, tn), lambda i, j: (0, j)),
            pl.BlockSpec((half, tn), lambda i, j: (0, j)),
            pl.BlockSpec((tm, tn), lambda i, j: (i, j)),
            pl.BlockSpec((None, 1, tn), lambda i, j: (i // tiles_per_batch, 0, j)),
        ],
        out_specs=pl.BlockSpec((tm, tn), lambda i, j: (i, j)),
        out_shape=jax.ShapeDtypeStruct((m, d), F32),
        compiler_params=_cparams(("arbitrary", "arbitrary")),
        name="outproj",
    )(attn, rw, w_a, w_r, x2, gt1)


def _ffn_kernel(x_ref, sc_ref, sh_ref, gt_ref, g2_ref, gf_ref, wg_ref, wu_ref, wd_ref, o_ref,
                h_scr, acc_scr):
    f = pl.program_id(1)

    def rms(x, g):
        return x * lax.rsqrt(jnp.mean(x * x, axis=-1, keepdims=True) + RMS_EPS) * g

    @pl.when(f == 0)
    def _():
        h = rms(x_ref[...], g2_ref[...]) * (1.0 + sc_ref[...]) + sh_ref[...]
        h_scr[...] = h.astype(BF16)
        acc_scr[...] = jnp.zeros_like(acc_scr)

    h = h_scr[...]
    gate = jnp.dot(h, wg_ref[...], preferred_element_type=F32)
    up = jnp.dot(h, wu_ref[...], preferred_element_type=F32)
    act = (gate * _sigmoid(gate) * up).astype(BF16)
    acc_scr[...] += jnp.dot(act, wd_ref[...], preferred_element_type=F32)

    @pl.when(f == pl.num_programs(1) - 1)
    def _():
        y = x_ref[...] + gt_ref[...] * acc_scr[...]
        o_ref[...] = rms(y, gf_ref[...])


def _ffn(x1, sc2, sh2, gt2, g2, gf, w_gate, w_up, w_down, *, seq, tm, tf):
    m, d = x1.shape
    fdim = w_gate.shape[1]
    tiles_per_batch = seq // tm
    vec = pl.BlockSpec((None, 1, d), lambda i, f: (i // tiles_per_batch, 0, 0))
    par = pl.BlockSpec((1, d), lambda i, f: (0, 0))
    return pl.pallas_call(
        _ffn_kernel,
        grid=(m // tm, fdim // tf),
        in_specs=[
            pl.BlockSpec((tm, d), lambda i, f: (i, 0)),
            vec, vec, vec, par, par,
            pl.BlockSpec((d, tf), lambda i, f: (0, f)),
            pl.BlockSpec((d, tf), lambda i, f: (0, f)),
            pl.BlockSpec((tf, d), lambda i, f: (f, 0)),
        ],
        out_specs=pl.BlockSpec((tm, d), lambda i, f: (i, 0)),
        out_shape=jax.ShapeDtypeStruct((m, d), F32),
        scratch_shapes=[pltpu.VMEM((tm, d), BF16), pltpu.VMEM((tm, d), F32)],
        compiler_params=_cparams(("arbitrary", "arbitrary")),
        name="ffn",
    )(x1, sc2, sh2, gt2, g2, gf, w_gate, w_up, w_down)


def _largest_tile(total, cap, step):
    t = min(total, cap)
    t -= t % step
    while total % t:
        t -= step
    return t


def kernel(x, c, positions, w_ada, b_ada, norm1_g, norm2_g, normf_g, w_in, w_out, mu_shift, w0,
           w_decay_up, a0, w_iclr_up, w_gate_up, k_k, k_a, r_k, gn_g, gn_b, w_ffn_gate,
           w_ffn_up, w_ffn_down):
    b, s, d = x.shape
    depth = w_ada.shape[0]
    assert depth == 1, "the final norm is fused into the (single) channel mixer"
    half = d // 2
    n_pairs = half // LANES
    m = b * s
    assert s % SPAN == 0 and half % LANES == 0
    tm = _largest_tile(s, 512, 8)
    tt = _largest_tile(s, 512, RWKV_GROUP * CHUNK)
    fdim = w_ffn_gate.shape[-1]
    tf = _largest_tile(fdim, 512, LANES)

    posb = jnp.broadcast_to(positions.astype(F32).reshape(m, 1), (m, LANES))
    x2 = x.reshape(m, d)
    c8 = jnp.pad(c, ((0, 8 - b % 8 if b % 8 else 0), (0, 0)))

    def lora_cols(t, axis):
        parts = jnp.split(t, [DECAY_LORA, DECAY_LORA + ICLR_LORA], axis=axis)
        offs = (0, LANES, 2 * LANES)
        shape = list(t.shape)
        shape[axis] = LORA_PAD
        out = jnp.zeros(shape, t.dtype)
        for part, off in zip(parts, offs):
            idx = [slice(None)] * t.ndim
            idx[axis] = slice(off, off + part.shape[axis])
            out = out.at[tuple(idx)].set(part)
        return out

    def pad_rows(t, rows):
        return jnp.pad(t, ((0, rows - t.shape[0]), (0, 0)))

    for i in range(depth):
        ada = _ada(c8, w_ada[i], b_ada[i][None, :])[:b]
        sh1, sc1, gt1, sh2, sc2, gt2 = (t[:, None, :] for t in jnp.split(ada, 6, axis=-1))

        w_main = w_in[i][:, :6 * half].astype(BF16)
        w_lora = lora_cols(w_in[i][:, 6 * half:], 1).astype(BF16)
        mu_main = mu_shift[i][None, :3 * half]
        mu_lora = lora_cols(mu_shift[i][None, 3 * half:], 1)
        p2, l2 = _inproj(x2, posb, sc1, sh1, norm1_g[i][None, :], w_main, w_lora,
                         mu_main, mu_lora, seq=s, tm=tm)
        p3 = p2.reshape(b, s, 6 * half)
        l3 = l2.reshape(b, s, LORA_PAD)

        attn = _attention(p3, n_pairs=n_pairs)

        cp = jnp.stack([w0[i], a0[i], k_k[i], k_a[i], r_k[i].reshape(-1), gn_g[i], gn_b[i],
                        jnp.zeros_like(w0[i])])
        rw = _rwkv(p3, l3, pad_rows(w_decay_up[i], LANES), pad_rows(w_iclr_up[i], LANES),
                   pad_rows(w_gate_up[i], 2 * LANES), cp, n_pairs=n_pairs, tt=tt,
                   group=RWKV_GROUP)

        wo = w_out[i].astype(BF16)
        x1 = _outproj(attn.reshape(m, half), rw.reshape(m, half), wo[:half], wo[half:], x2, gt1,
                      seq=s, tm=tm)
        x2 = _ffn(x1, sc2, sh2, gt2, norm2_g[i][None, :], normf_g[None, :],
                  w_ffn_gate[i].astype(BF16),
                  w_ffn_up[i].astype(BF16), w_ffn_down[i].astype(BF16), seq=s, tm=tm, tf=tf)
    return x2.reshape(b, s, d)
```

```python
import functools
import math

import jax
import jax.numpy as jnp
from jax import lax
from jax.experimental import pallas as pl
from jax.experimental.pallas import tpu as pltpu

HEAD_DIM = 64
LANES = 128
ROT_DIM = HEAD_DIM // 4
ROPE_THETA = 500000.0
RMS_EPS = 1e-6
GN_EPS = HEAD_DIM * 1e-5
DILATIONS = (1, 4, 16)
WINDOW_BLOCK = 128
SPAN = WINDOW_BLOCK * max(DILATIONS)
DECAY_LORA = 64
ICLR_LORA = 64
GATE_LORA = 160
LORA_PAD = 512
CHUNK = 64
ATTN_GROUP = 8
RWKV_GROUP = 8
NEG = -1e30
VMEM_LIMIT = 56 * 1024 * 1024

F32 = jnp.float32
BF16 = jnp.bfloat16
HI = lax.Precision.HIGHEST


def _cparams(sem):
    return pltpu.CompilerParams(dimension_semantics=sem, vmem_limit_bytes=VMEM_LIMIT)


def _sigmoid(z):
    return 1.0 / (1.0 + jnp.exp(-z))


def _ada_kernel(c_ref, w_ref, b_ref, o_ref):
    c = c_ref[...]
    s = c * _sigmoid(c)
    o_ref[...] = jnp.dot(s, w_ref[...], precision=HI, preferred_element_type=F32) + b_ref[...]


def _ada(c8, w_ada, b_ada):
    rows, d = c8.shape
    n = w_ada.shape[1]
    tn = _largest_tile(n, 1024, LANES)
    return pl.pallas_call(
        _ada_kernel,
        grid=(n // tn,),
        in_specs=[
            pl.BlockSpec((rows, d), lambda j: (0, 0)),
            pl.BlockSpec((d, tn), lambda j: (0, j)),
            pl.BlockSpec((1, tn), lambda j: (0, j)),
        ],
        out_specs=pl.BlockSpec((rows, tn), lambda j: (0, j)),
        out_shape=jax.ShapeDtypeStruct((rows, n), F32),
        compiler_params=_cparams(("arbitrary",)),
        name="ada",
    )(c8, w_ada, b_ada)


def _inproj_kernel(x_ref, pos_ref, sc_ref, sh_ref, g_ref, w_ref, wl_ref, mu_ref, mul_ref,
                   p_ref, l_ref, last_scr, lastl_scr, *, tiles_per_batch, n_attn_tiles):
    i = pl.program_id(0)
    tm = x_ref.shape[0]
    n_tiles = last_scr.shape[0] + n_attn_tiles
    tn = p_ref.shape[1] // n_tiles
    first_in_batch = (i % tiles_per_batch) == 0

    x = x_ref[...]
    ms = jnp.mean(x * x, axis=-1, keepdims=True)
    y = x * lax.rsqrt(ms + RMS_EPS) * g_ref[...]
    h = (y * (1.0 + sc_ref[...]) + sh_ref[...]).astype(BF16)
    lane = lax.broadcasted_iota(jnp.int32, (1, LANES), 1)
    freq = (lane % (ROT_DIM // 2)).astype(F32)
    inv = jnp.exp(freq * (-2.0 / ROT_DIM * math.log(ROPE_THETA)))
    ang = pos_ref[...] * inv
    cs = jnp.cos(ang)
    sn = jnp.sin(ang)
    lm = lane % HEAD_DIM
    ct = jnp.where(lm < ROT_DIM, cs, 1.0)
    sa = jnp.where(lm < ROT_DIM // 2, -sn, 0.0)
    sb = jnp.where((lm >= ROT_DIM // 2) & (lm < ROT_DIM), sn, 0.0)

    def shifted(val, carry_row, mu):
        prev = pltpu.roll(val, 1, 0)
        row = lax.broadcasted_iota(jnp.int32, val.shape, 0)
        carry = jnp.where(first_in_batch, 0.0, carry_row)
        prev = jnp.where(row == 0, carry, prev)
        return val + (prev - val) * mu

    for jt in range(n_tiles):
        res = jnp.dot(h, w_ref[:, jt * tn:(jt + 1) * tn], preferred_element_type=F32)
        if jt < 2:
            for gidx in range(tn // LANES):
                t = res[:, gidx * LANES:(gidx + 1) * LANES]
                up = pltpu.roll(t, LANES - ROT_DIM // 2, 1)
                dn = pltpu.roll(t, ROT_DIM // 2, 1)
                c0 = jt * tn + gidx * LANES
                p_ref[:, c0:c0 + LANES] = t * ct + up * sa + dn * sb
        elif jt < n_attn_tiles:
            p_ref[:, jt * tn:(jt + 1) * tn] = res
        else:
            jj = jt - n_attn_tiles
            p_ref[:, jt * tn:(jt + 1) * tn] = shifted(res, last_scr[jj, 7:8, :],
                                                     mu_ref[:, jj * tn:(jj + 1) * tn])
            last_scr[jj] = res[tm - 8:, :]

    resl = jnp.dot(h, wl_ref[...], preferred_element_type=F32)
    yl = shifted(resl, lastl_scr[7:8, :], mul_ref[...])
    lastl_scr[...] = resl[tm - 8:, :]
    l_ref[:, 0:LANES] = jnp.tanh(yl[:, 0:LANES])
    l_ref[:, LANES:2 * LANES] = yl[:, LANES:2 * LANES]
    l_ref[:, 2 * LANES:] = _sigmoid(yl[:, 2 * LANES:])


def _inproj(x2, posb, sc1, sh1, g1, w_main, w_lora, mu_main, mu_lora, *, seq, tm):
    m, d = x2.shape
    n = w_main.shape[1]
    tn = n // 6
    n_attn_tiles = 3
    tiles_per_batch = seq // tm
    kern = functools.partial(_inproj_kernel, tiles_per_batch=tiles_per_batch,
                             n_attn_tiles=n_attn_tiles)
    once = pl.Buffered(1)
    return pl.pallas_call(
        kern,
        grid=(m // tm,),
        in_specs=[
            pl.BlockSpec((tm, d), lambda i: (i, 0)),
            pl.BlockSpec((tm, LANES), lambda i: (i, 0)),
            pl.BlockSpec((None, 1, d), lambda i: (i // tiles_per_batch, 0, 0)),
            pl.BlockSpec((None, 1, d), lambda i: (i // tiles_per_batch, 0, 0)),
            pl.BlockSpec((1, d), lambda i: (0, 0)),
            pl.BlockSpec((d, n), lambda i: (0, 0), pipeline_mode=once),
            pl.BlockSpec((d, LORA_PAD), lambda i: (0, 0), pipeline_mode=once),
            pl.BlockSpec((1, n - n_attn_tiles * tn), lambda i: (0, 0)),
            pl.BlockSpec((1, LORA_PAD), lambda i: (0, 0)),
        ],
        out_specs=[
            pl.BlockSpec((tm, n), lambda i: (i, 0)),
            pl.BlockSpec((tm, LORA_PAD), lambda i: (i, 0)),
        ],
        out_shape=[
            jax.ShapeDtypeStruct((m, n), F32),
            jax.ShapeDtypeStruct((m, LORA_PAD), F32),
        ],
        scratch_shapes=[
            pltpu.VMEM((6 - n_attn_tiles, 8, tn), F32),
            pltpu.VMEM((8, LORA_PAD), F32),
        ],
        compiler_params=_cparams(("arbitrary",)),
        name="inproj",
    )(x2, posb, sc1, sh1, g1, w_main, w_lora, mu_main, mu_lora)


def _attn_kernel(q_ref, kc_ref, kp_ref, vc_ref, vp_ref, o_ref, pv_scr, m_scr, l_scr):
    blk = WINDOW_BLOCK
    has_prev = pl.program_id(2) > 0
    lane = lax.broadcasted_iota(jnp.int32, (1, LANES), 1)
    head0 = lane < HEAD_DIM
    qi = lax.broadcasted_iota(jnp.int32, (blk, 2 * blk), 0)
    kj = lax.broadcasted_iota(jnp.int32, (blk, 2 * blk), 1)
    in_cur = kj >= blk
    slack = jnp.where(in_cur, qi - (kj - blk), kj - qi)
    bias_full = jnp.where(slack >= 0, 0.0, NEG)
    bias_cur = jnp.where(in_cur, bias_full, NEG)
    bias_edge = jnp.where(has_prev, bias_full, bias_cur)
    cur = (kc_ref, vc_ref)
    prv = (kp_ref, vp_ref)

    def run_blocks(d, specs, emit):
        rows = lambda start: pl.ds(start, blk, stride=d) if d > 1 else pl.ds(start, blk)
        qs = [q_ref[rows(q0), :] * (HEAD_DIM ** -0.5) for q0, _, _, _ in specs]
        kw = [jnp.concatenate([pr[0][rows(p0), :], kc_ref[rows(q0), :]], axis=0).astype(BF16)
              for q0, pr, p0, _ in specs]
        vw = [jnp.concatenate([pr[1][rows(p0), :], vc_ref[rows(q0), :]], axis=0).astype(BF16)
              for q0, pr, p0, _ in specs]
        qh = [jnp.where(mh, q, 0.0).astype(BF16) for q in qs
              for mh in (head0, jnp.logical_not(head0))]
        two = lambda xs: [x for x in xs for _ in range(2)]
        s = [lax.dot_general(q, k, (((1,), (1,)), ((), ())), preferred_element_type=F32) + b
             for q, k, b in zip(qh, two(kw), two([sp[3] for sp in specs]))]
        mx = [jnp.max(x, axis=-1, keepdims=True) for x in s]
        p = [jnp.exp(x - m_) for x, m_ in zip(s, mx)]
        den = [jnp.sum(x, axis=-1, keepdims=True) for x in p]
        pv = [jnp.dot(x.astype(BF16), v, preferred_element_type=F32) for x, v in zip(p, two(vw))]
        for n, (q0, _, _, _) in enumerate(specs):
            emit(rows(q0), jnp.where(head0, pv[2 * n], pv[2 * n + 1]),
                 jnp.where(head0, mx[2 * n], mx[2 * n + 1]),
                 jnp.where(head0, den[2 * n], den[2 * n + 1]))

    def groups(blocks):
        return [blocks[n:n + ATTN_GROUP] for n in range(0, len(blocks), ATTN_GROUP)]

    for slot, d in ((0, DILATIONS[1]), (1, DILATIONS[2])):
        unit = blk * d
        n_units = SPAN // unit

        def park(rr, pv, mx, den, slot=slot):
            pv_scr[slot, rr, :] = pv
            m_scr[slot, rr, :] = mx
            l_scr[slot, rr, :] = den

        blocks = [(r, prv, (n_units - 1) * unit + r, bias_edge) for r in range(d)]
        blocks += [(n * unit + r, cur, (n - 1) * unit + r, bias_full)
                   for n in range(1, n_units) for r in range(d)]
        for grp in groups(blocks):
            run_blocks(d, grp, park)

    def finish(rr, pv, mx, den):
        m4, m16 = m_scr[0, rr, :], m_scr[1, rr, :]
        mm = jnp.maximum(mx, jnp.maximum(m4, m16))
        e1, e4, e16 = jnp.exp(mx - mm), jnp.exp(m4 - mm), jnp.exp(m16 - mm)
        num = e1 * pv + e4 * pv_scr[0, rr, :] + e16 * pv_scr[1, rr, :]
        dn = e1 * den + e4 * l_scr[0, rr, :] + e16 * l_scr[1, rr, :]
        o_ref[rr, :] = (num / dn).astype(o_ref.dtype)

    blocks = [(0, prv, SPAN - blk, bias_edge)]
    blocks += [(n * blk, cur, (n - 1) * blk, bias_full) for n in range(1, SPAN // blk)]
    for grp in groups(blocks):
        run_blocks(1, grp, finish)


def _attention(p3, *, n_pairs):
    b, s, _ = p3.shape
    n_spans = s // SPAN

    def spec(col0, prev):
        if prev:
            return pl.BlockSpec((None, SPAN, LANES),
                                lambda bi, hp, si: (bi, jnp.maximum(si - 1, 0), col0 + hp))
        return pl.BlockSpec((None, SPAN, LANES), lambda bi, hp, si: (bi, si, col0 + hp))

    return pl.pallas_call(
        _attn_kernel,
        grid=(b, n_pairs, n_spans),
        in_specs=[spec(0, False), spec(n_pairs, False), spec(n_pairs, True),
                  spec(2 * n_pairs, False), spec(2 * n_pairs, True)],
        out_specs=pl.BlockSpec((None, SPAN, LANES), lambda bi, hp, si: (bi, si, hp)),
        out_shape=jax.ShapeDtypeStruct((b, s, n_pairs * LANES), BF16),
        scratch_shapes=[pltpu.VMEM((2, SPAN, LANES), F32)] * 3,
        compiler_params=_cparams(("arbitrary", "arbitrary", "arbitrary")),
        name="attn",
    )(p3, p3, p3, p3, p3)


def _unit_lower_inverse(a, masks, bd):
    eye, diag4, offs = masks
    mm = lambda xs, ys: [jnp.dot(x.astype(BF16), bd(y).astype(BF16), preferred_element_type=F32)
                         for x, y in zip(xs, ys)]
    ad = [jnp.where(diag4, x, 0.0) for x in a]
    a2 = mm(ad, ad)
    a3 = mm(a2, ad)
    t = [eye + x1 + x2 + x3 for x1, x2, x3 in zip(ad, a2, a3)]
    for off in offs:
        inner = mm([jnp.where(off, x, 0.0) for x in a], t)
        t = [x + y for x, y in zip(t, mm(t, inner))]
    return t


def _rwkv_kernel(r_ref, k_ref, v_ref, l_ref, wd_ref, wa_ref, wg_ref, cp_ref, o_ref, st_scr,
                 *, n_groups, group):
    n_streams = r_ref.shape[0]

    @pl.when(pl.program_id(1) == 0)
    def _():
        st_scr[...] = jnp.zeros_like(st_scr)

    c = CHUNK
    lane = lax.broadcasted_iota(jnp.int32, (1, LANES), 1)
    head0 = lane < HEAD_DIM
    ri = lax.broadcasted_iota(jnp.int32, (c, 2 * c), 0)
    cj = lax.broadcasted_iota(jnp.int32, (c, 2 * c), 1) % c
    same = lambda size: (ri // size) == (cj // size)
    strict = cj < ri
    incl = cj <= ri
    eye = (ri == cj).astype(F32)
    sizes = []
    size = 4
    while size < c:
        sizes.append(size)
        size *= 2
    inv_masks = (eye, same(4), [same(2 * sz) & jnp.logical_not(same(sz)) for sz in sizes])
    tri = (lax.broadcasted_iota(jnp.int32, (c, c), 1)
           <= lax.broadcasted_iota(jnp.int32, (c, c), 0)).astype(BF16)
    r2 = lax.broadcasted_iota(jnp.int32, (LANES, LANES), 0)
    c2 = lax.broadcasted_iota(jnp.int32, (LANES, LANES), 1)
    same_head = (r2 // HEAD_DIM) == (c2 // HEAD_DIM)

    cp = cp_ref[...]
    w0, a0, k_k, k_a, r_k, gn_g, gn_b = (cp[n:n + 1, :] for n in range(7))
    wd = wd_ref[...]
    wa = wa_ref[...]
    wg = wg_ref[...]

    def head_sum(x):
        s0 = jnp.sum(jnp.where(head0, x, 0.0), axis=-1, keepdims=True)
        s1 = jnp.sum(jnp.where(head0, 0.0, x), axis=-1, keepdims=True)
        return jnp.where(head0, s0, s1)

    def bd(x):
        wide = x.shape[1] // LANES
        m0 = head0 if wide == 1 else jnp.concatenate([head0] * wide, axis=1)
        return jnp.concatenate([jnp.where(m0, x, 0.0), jnp.where(m0, 0.0, x)], axis=0)

    def bdot(x, y):
        return jnp.dot(x.astype(BF16), y.astype(BF16), preferred_element_type=F32)

    def bdot_t(x, y):
        return lax.dot_general(x.astype(BF16), y.astype(BF16), (((1,), (1,)), ((), ())),
                               preferred_element_type=F32)

    def tdot(x, y):
        return lax.dot_general(x.astype(BF16), y.astype(BF16), (((0,), (0,)), ((), ())),
                               preferred_element_type=F32)

    def split(x):
        hi = x.astype(BF16)
        return hi, (x - hi.astype(F32)).astype(BF16)

    chunks = lambda x: [x[n * c:(n + 1) * c] for n in range(group)]
    each = lambda f, *xs: [f(*args) for args in zip(*xs)]

    def prepare(bi, rows):
        r = r_ref[bi, rows, :]
        k = k_ref[bi, rows, :]
        v = v_ref[bi, rows, :]
        th = l_ref[bi, rows, 0:LANES]
        al = l_ref[bi, rows, LANES:2 * LANES]
        sg = l_ref[bi, rows, 2 * LANES:]
        z = w0 + bdot(th, wd)
        w = jnp.minimum(z, 0.0) - jnp.log(1.0 + jnp.exp(-jnp.abs(z))) - 0.5
        ld = -jnp.exp(w)
        a = _sigmoid(a0 + bdot(al, wa))
        g = bdot(sg, wg)
        kk = k * k_k
        kk = kk / jnp.maximum(jnp.sqrt(head_sum(kk * kk)), 1e-12)
        kp = k * (1.0 + (a - 1.0) * k_a)
        av = -kk
        bv = kk * a
        bonus = head_sum(r * kp * r_k) * v
        ld_hi, ld_lo = split(ld)
        lc = jnp.concatenate(
            each(lambda hi, lo: (jnp.dot(tri, hi, preferred_element_type=F32)
                                 + jnp.dot(tri, lo, preferred_element_type=F32)),
                 chunks(ld_hi), chunks(ld_lo)), axis=0)
        ltot = [x[c - 1:c, :] for x in chunks(lc)]
        ltot_rows = jnp.concatenate([jnp.broadcast_to(x, (c, LANES)) for x in ltot], axis=0)
        e_nl = jnp.exp(-lc)
        to_end = jnp.exp(ltot_rows - lc)
        return dict(rt=chunks(r * jnp.exp(lc)), kt=chunks(kp * e_nl), bt=chunks(bv * e_nl),
                    at=chunks(av * jnp.exp(lc - ld)), kh=chunks(kp * to_end),
                    bh=chunks(bv * to_end), v=chunks(v), decay=[jnp.exp(x) for x in ltot],
                    bonus=chunks(bonus), g=chunks(g))

    def solve(q):
        aa = each(lambda a_, r_, b_, k_: bdot_t(jnp.concatenate([a_, r_], axis=0),
                                                jnp.concatenate([bd(b_), bd(k_)], axis=0)),
                  q["at"], q["rt"], q["bt"], q["kt"])
        a_ab = [jnp.where(strict, x[:c, :LANES], 0.0) for x in aa]
        a_ak = [jnp.where(strict, x[:c, LANES:], 0.0) for x in aa]
        a_rb = [jnp.where(incl, x[c:, :LANES], 0.0) for x in aa]
        a_rk = [jnp.where(incl, x[c:, LANES:], 0.0) for x in aa]
        av = each(lambda a_, v_: bdot(a_, bd(v_)), a_ak, q["v"])
        t_inv = _unit_lower_inverse(a_ab, inv_masks, bd)
        y = each(lambda t_, a_, av_: bdot(t_, bd(jnp.concatenate([a_, av_], axis=1))),
                 t_inv, q["at"], av)
        a_st = [x[:, :LANES] for x in y]
        u_v = [x[:, LANES:] for x in y]
        zz = each(lambda rb_, rk_, y_, v_: bdot(
            jnp.concatenate([rb_, rk_], axis=1),
            jnp.concatenate([bd(y_), bd(jnp.concatenate([jnp.zeros_like(v_), v_], axis=1))], axis=0)),
            a_rb, a_rk, y, q["v"])
        r_st = [x + z_[:, :LANES] for x, z_ in zip(q["rt"], zz)]
        o_v = [z_[:, LANES:] for z_ in zz]
        trans = each(lambda a_, b_: jnp.where(same_head, tdot(a_, b_), 0.0), a_st, q["bh"])

        def injected(u_, v_, b_, k_):
            full = tdot(jnp.concatenate([u_, v_], axis=0), jnp.concatenate([b_, k_], axis=0))
            return jnp.where(head0, full[:HEAD_DIM], full[HEAD_DIM:])
        inject = each(injected, u_v, q["v"], q["bh"], q["kh"])
        return r_st, o_v, trans, inject

    def group_body(gi, _):
        base = pl.multiple_of(gi * (group * c), group * c)
        q = {}
        for bi in range(n_streams):
            for name, vals in prepare(bi, pl.ds(base, group * c)).items():
                q.setdefault(name, []).extend(vals)
        r_st, o_v, trans, inject = solve(q)
        st = [st_scr[bi] for bi in range(n_streams)]
        for n in range(group):
            idx = [bi * group + n for bi in range(n_streams)]
            o = [bdot_t(r_st[i], bd(s)) + o_v[i] for i, s in zip(idx, st)]
            parts = [split(s) for s in st]
            tb = [trans[i].astype(BF16) for i in idx]
            hi = [jnp.dot(p_[0], t_, preferred_element_type=F32) for p_, t_ in zip(parts, tb)]
            lo = [jnp.dot(p_[1], t_, preferred_element_type=F32) for p_, t_ in zip(parts, tb)]
            st = [s * q["decay"][i] + h_ + l_ + inject[i]
                  for s, i, h_, l_ in zip(st, idx, hi, lo)]
            rows = pl.ds(pl.multiple_of(base + n * c, c), c)
            for bi, i in enumerate(idx):
                mean = head_sum(o[bi]) * (1.0 / HEAD_DIM)
                dev = o[bi] - mean
                var = head_sum(dev * dev) * (1.0 / HEAD_DIM)
                on = dev * lax.rsqrt(var + GN_EPS) * gn_g + gn_b
                o_ref[bi, rows, :] = ((on + q["bonus"][i]) * q["g"][i]).astype(o_ref.dtype)
        for bi in range(n_streams):
            st_scr[bi] = st[bi]
        return 0

    lax.fori_loop(0, n_groups, group_body, 0)


def _rwkv(p3, l3, wd, wa, wg, cp, *, n_pairs, tt, group):
    b, s, _ = p3.shape
    col0 = 3 * n_pairs

    def pspec(off):
        return pl.BlockSpec((b, tt, LANES), lambda hp, ti: (0, ti, col0 + off + hp))

    wspec = lambda rows: pl.BlockSpec((rows, LANES), lambda hp, ti: (0, hp))
    kern = functools.partial(_rwkv_kernel, n_groups=tt // (group * CHUNK), group=group)
    return pl.pallas_call(
        kern,
        grid=(n_pairs, s // tt),
        in_specs=[pspec(0), pspec(n_pairs), pspec(2 * n_pairs),
                  pl.BlockSpec((b, tt, LORA_PAD), lambda hp, ti: (0, ti, 0)),
                  wspec(LANES), wspec(LANES), wspec(2 * LANES), wspec(8)],
        out_specs=pl.BlockSpec((b, tt, LANES), lambda hp, ti: (0, ti, hp)),
        out_shape=jax.ShapeDtypeStruct((b, s, n_pairs * LANES), BF16),
        scratch_shapes=[pltpu.VMEM((b, HEAD_DIM, LANES), F32)],
        compiler_params=_cparams(("arbitrary", "arbitrary")),
        name="rwkv",
    )(p3, p3, p3, l3, wd, wa, wg, cp)


def _outproj_kernel(a_ref, r_ref, wa_ref, wr_ref, x_ref, gt_ref, o_ref):
    mix = jnp.dot(a_ref[...], wa_ref[...], preferred_element_type=F32)
    mix += jnp.dot(r_ref[...], wr_ref[...], preferred_element_type=F32)
    o_ref[...] = x_ref[...] + gt_ref[...] * mix


def _outproj(attn, rw, w_a, w_r, x2, gt1, *, seq, tm):
    m, d = x2.shape
    half = attn.shape[1]
    tn = d // 2
    tiles_per_batch = seq // tm
    return pl.pallas_call(
        _outproj_kernel,
        grid=(m // tm, d // tn),
        in_specs=[
            pl.BlockSpec((tm, half), lambda i, j: (i, 0)),
            pl.BlockSpec((tm, half), lambda i, j: (i, 0)),
            pl.BlockSpec((half, tn), lambda i, j: (0, j)),
            pl.BlockSpec((half, tn), lambda i, j: (0, j)),
            pl.BlockSpec((tm, tn), lambda i, j: (i, j)),
            pl.BlockSpec((None, 1, tn), lambda i, j: (i // tiles_per_batch, 0, j)),
        ],
        out_specs=pl.BlockSpec((tm, tn), lambda i, j: (i, j)),
        out_shape=jax.ShapeDtypeStruct((m, d), F32),
        compiler_params=_cparams(("arbitrary", "arbitrary")),
        name="outproj",
    )(attn, rw, w_a, w_r, x2, gt1)


def _ffn_kernel(x_ref, sc_ref, sh_ref, gt_ref, g2_ref, gf_ref, wg_ref, wu_ref, wd_ref, o_ref,
                h_scr, acc_scr):
    f = pl.program_id(1)

    def rms(x, g):
        return x * lax.rsqrt(jnp.mean(x * x, axis=-1, keepdims=True) + RMS_EPS) * g

    @pl.when(f == 0)
    def _():
        h = rms(x_ref[...], g2_ref[...]) * (1.0 + sc_ref[...]) + sh_ref[...]
        h_scr[...] = h.astype(BF16)
        acc_scr[...] = jnp.zeros_like(acc_scr)

    h = h_scr[...]
    gate = jnp.dot(h, wg_ref[...], preferred_element_type=F32)
    up = jnp.dot(h, wu_ref[...], preferred_element_type=F32)
    act = (gate * _sigmoid(gate) * up).astype(BF16)
    acc_scr[...] += jnp.dot(act, wd_ref[...], preferred_element_type=F32)

    @pl.when(f == pl.num_programs(1) - 1)
    def _():
        y = x_ref[...] + gt_ref[...] * acc_scr[...]
        o_ref[...] = rms(y, gf_ref[...])


def _ffn(x1, sc2, sh2, gt2, g2, gf, w_gate, w_up, w_down, *, seq, tm, tf):
    m, d = x1.shape
    fdim = w_gate.shape[1]
    tiles_per_batch = seq // tm
    vec = pl.BlockSpec((None, 1, d), lambda i, f: (i // tiles_per_batch, 0, 0))
    par = pl.BlockSpec((1, d), lambda i, f: (0, 0))
    return pl.pallas_call(
        _ffn_kernel,
        grid=(m // tm, fdim // tf),
        in_specs=[
            pl.BlockSpec((tm, d), lambda i, f: (i, 0)),
            vec, vec, vec, par, par,
            pl.BlockSpec((d, tf), lambda i, f: (0, f)),
            pl.BlockSpec((d, tf), lambda i, f: (0, f)),
            pl.BlockSpec((tf, d), lambda i, f: (f, 0)),
        ],
        out_specs=pl.BlockSpec((tm, d), lambda i, f: (i, 0)),
        out_shape=jax.ShapeDtypeStruct((m, d), F32),
        scratch_shapes=[pltpu.VMEM((tm, d), BF16), pltpu.VMEM((tm, d), F32)],
        compiler_params=_cparams(("arbitrary", "arbitrary")),
        name="ffn",
    )(x1, sc2, sh2, gt2, g2, gf, w_gate, w_up, w_down)


def _largest_tile(total, cap, step):
    t = min(total, cap)
    t -= t % step
    while total % t:
        t -= step
    return t


def kernel(x, c, positions, w_ada, b_ada, norm1_g, norm2_g, normf_g, w_in, w_out, mu_shift, w0,
           w_decay_up, a0, w_iclr_up, w_gate_up, k_k, k_a, r_k, gn_g, gn_b, w_ffn_gate,
           w_ffn_up, w_ffn_down):
    b, s, d = x.shape
    depth = w_ada.shape[0]
    assert depth == 1, "the final norm is fused into the (single) channel mixer"
    half = d // 2
    n_pairs = half // LANES
    m = b * s
    assert s % SPAN == 0 and half % LANES == 0
    tm = _largest_tile(s, 512, 8)
    tt = _largest_tile(s, 512, RWKV_GROUP * CHUNK)
    fdim = w_ffn_gate.shape[-1]
    tf = _largest_tile(fdim, 512, LANES)

    posb = jnp.broadcast_to(positions.astype(F32).reshape(m, 1), (m, LANES))
    x2 = x.reshape(m, d)
    c8 = jnp.pad(c, ((0, 8 - b % 8 if b % 8 else 0), (0, 0)))

    def lora_cols(t, axis):
        parts = jnp.split(t, [DECAY_LORA, DECAY_LORA + ICLR_LORA], axis=axis)
        offs = (0, LANES, 2 * LANES)
        shape = list(t.shape)
        shape[axis] = LORA_PAD
        out = jnp.zeros(shape, t.dtype)
        for part, off in zip(parts, offs):
            idx = [slice(None)] * t.ndim
            idx[axis] = slice(off, off + part.shape[axis])
            out = out.at[tuple(idx)].set(part)
        return out

    def pad_rows(t, rows):
        return jnp.pad(t, ((0, rows - t.shape[0]), (0, 0)))

    for i in range(depth):
        ada = _ada(c8, w_ada[i], b_ada[i][None, :])[:b]
        sh1, sc1, gt1, sh2, sc2, gt2 = (t[:, None, :] for t in jnp.split(ada, 6, axis=-1))

        w_main = w_in[i][:, :6 * half].astype(BF16)
        w_lora = lora_cols(w_in[i][:, 6 * half:], 1).astype(BF16)
        mu_main = mu_shift[i][None, :3 * half]
        mu_lora = lora_cols(mu_shift[i][None, 3 * half:], 1)
        p2, l2 = _inproj(x2, posb, sc1, sh1, norm1_g[i][None, :], w_main, w_lora,
                         mu_main, mu_lora, seq=s, tm=_largest_tile(s, 256, 8))
        p3 = p2.reshape(b, s, 6 * half)
        l3 = l2.reshape(b, s, LORA_PAD)

        attn = _attention(p3, n_pairs=n_pairs)

        cp = jnp.stack([w0[i], a0[i], k_k[i], k_a[i], r_k[i].reshape(-1), gn_g[i], gn_b[i],
                        jnp.zeros_like(w0[i])])
        rw = _rwkv(p3, l3, pad_rows(w_decay_up[i], LANES), pad_rows(w_iclr_up[i], LANES),
                   pad_rows(w_gate_up[i], 2 * LANES), cp, n_pairs=n_pairs, tt=tt,
                   group=RWKV_GROUP)

        wo = w_out[i].astype(BF16)
        x1 = _outproj(attn.reshape(m, half), rw.reshape(m, half), wo[:half], wo[half:], x2, gt1,
                      seq=s, tm=tm)
        x2 = _ffn(x1, sc2, sh2, gt2, norm2_g[i][None, :], normf_g[None, :],
                  w_ffn_gate[i].astype(BF16),
                  w_ffn_up[i].astype(BF16), w_ffn_down[i].astype(BF16), seq=s, tm=tm, tf=tf)
    return x2.reshape(b, s, d)
```

```python
import functools
import math

import jax
import jax.numpy as jnp
from jax import lax
from jax.experimental import pallas as pl
from jax.experimental.pallas import tpu as pltpu

HEAD_DIM = 64
LANES = 128
ROT_DIM = HEAD_DIM // 4
ROPE_THETA = 500000.0
RMS_EPS = 1e-6
GN_EPS = HEAD_DIM * 1e-5
DILATIONS = (1, 4, 16)
WINDOW_BLOCK = 128
SPAN = WINDOW_BLOCK * max(DILATIONS)
DECAY_LORA = 64
ICLR_LORA = 64
GATE_LORA = 160
LORA_PAD = 512
CHUNK = 64
ATTN_GROUP = 8
RWKV_GROUP = 8
NEG = -1e30
VMEM_LIMIT = 56 * 1024 * 1024
MIXFFN_VMEM_LIMIT = 60 * 1024 * 1024

F32 = jnp.float32
BF16 = jnp.bfloat16
HI = lax.Precision.HIGHEST


def _cparams(sem, vmem_limit=VMEM_LIMIT):
    return pltpu.CompilerParams(dimension_semantics=sem, vmem_limit_bytes=vmem_limit)


def _sigmoid(z):
    return 1.0 / (1.0 + jnp.exp(-z))


def _ada_kernel(c_ref, w_ref, b_ref, o_ref):
    c = c_ref[...]
    s = c * _sigmoid(c)
    o_ref[...] = jnp.dot(s, w_ref[...], precision=HI, preferred_element_type=F32) + b_ref[...]


def _ada(c8, w_ada, b_ada):
    rows, d = c8.shape
    n = w_ada.shape[1]
    tn = _largest_tile(n, 1024, LANES)
    return pl.pallas_call(
        _ada_kernel,
        grid=(n // tn,),
        in_specs=[
            pl.BlockSpec((rows, d), lambda j: (0, 0)),
            pl.BlockSpec((d, tn), lambda j: (0, j)),
            pl.BlockSpec((1, tn), lambda j: (0, j)),
        ],
        out_specs=pl.BlockSpec((rows, tn), lambda j: (0, j)),
        out_shape=jax.ShapeDtypeStruct((rows, n), F32),
        compiler_params=_cparams(("arbitrary",)),
        name="ada",
    )(c8, w_ada, b_ada)


def _inproj_kernel(x_ref, pos_ref, sc_ref, sh_ref, g_ref, w_ref, wl_ref, mu_ref, mul_ref,
                   p_ref, l_ref, last_scr, lastl_scr, *, tiles_per_batch, n_attn_tiles):
    i = pl.program_id(0)
    tm = x_ref.shape[0]
    n_tiles = last_scr.shape[0] + n_attn_tiles
    tn = p_ref.shape[1] // n_tiles
    first_in_batch = (i % tiles_per_batch) == 0

    x = x_ref[...]
    ms = jnp.mean(x * x, axis=-1, keepdims=True)
    y = x * lax.rsqrt(ms + RMS_EPS) * g_ref[...]
    h = (y * (1.0 + sc_ref[...]) + sh_ref[...]).astype(BF16)
    lane = lax.broadcasted_iota(jnp.int32, (1, LANES), 1)
    freq = (lane % (ROT_DIM // 2)).astype(F32)
    inv = jnp.exp(freq * (-2.0 / ROT_DIM * math.log(ROPE_THETA)))
    ang = pos_ref[...] * inv
    cs = jnp.cos(ang)
    sn = jnp.sin(ang)
    lm = lane % HEAD_DIM
    ct = jnp.where(lm < ROT_DIM, cs, 1.0)
    sa = jnp.where(lm < ROT_DIM // 2, -sn, 0.0)
    sb = jnp.where((lm >= ROT_DIM // 2) & (lm < ROT_DIM), sn, 0.0)

    def shifted(val, carry_row, mu):
        prev = pltpu.roll(val, 1, 0)
        row = lax.broadcasted_iota(jnp.int32, val.shape, 0)
        carry = jnp.where(first_in_batch, 0.0, carry_row)
        prev = jnp.where(row == 0, carry, prev)
        return val + (prev - val) * mu

    for jt in range(n_tiles):
        res = jnp.dot(h, w_ref[:, jt * tn:(jt + 1) * tn], preferred_element_type=F32)
        if jt < 2:
            for gidx in range(tn // LANES):
                t = res[:, gidx * LANES:(gidx + 1) * LANES]
                up = pltpu.roll(t, LANES - ROT_DIM // 2, 1)
                dn = pltpu.roll(t, ROT_DIM // 2, 1)
                c0 = jt * tn + gidx * LANES
                p_ref[:, c0:c0 + LANES] = t * ct + up * sa + dn * sb
        elif jt < n_attn_tiles:
            p_ref[:, jt * tn:(jt + 1) * tn] = res
        else:
            jj = jt - n_attn_tiles
            p_ref[:, jt * tn:(jt + 1) * tn] = shifted(res, last_scr[jj, 7:8, :],
                                                     mu_ref[:, jj * tn:(jj + 1) * tn])
            last_scr[jj] = res[tm - 8:, :]

    resl = jnp.dot(h, wl_ref[...], preferred_element_type=F32)
    yl = shifted(resl, lastl_scr[7:8, :], mul_ref[...])
    lastl_scr[...] = resl[tm - 8:, :]
    l_ref[:, 0:LANES] = jnp.tanh(yl[:, 0:LANES])
    l_ref[:, LANES:2 * LANES] = yl[:, LANES:2 * LANES]
    l_ref[:, 2 * LANES:] = _sigmoid(yl[:, 2 * LANES:])


def _inproj(x2, posb, sc1, sh1, g1, w_main, w_lora, mu_main, mu_lora, *, seq, tm):
    m, d = x2.shape
    n = w_main.shape[1]
    tn = n // 6
    n_attn_tiles = 3
    tiles_per_batch = seq // tm
    kern = functools.partial(_inproj_kernel, tiles_per_batch=tiles_per_batch,
                             n_attn_tiles=n_attn_tiles)
    once = pl.Buffered(1)
    return pl.pallas_call(
        kern,
        grid=(m // tm,),
        in_specs=[
            pl.BlockSpec((tm, d), lambda i: (i, 0)),
            pl.BlockSpec((tm, LANES), lambda i: (i, 0)),
            pl.BlockSpec((None, 1, d), lambda i: (i // tiles_per_batch, 0, 0)),
            pl.BlockSpec((None, 1, d), lambda i: (i // tiles_per_batch, 0, 0)),
            pl.BlockSpec((1, d), lambda i: (0, 0)),
            pl.BlockSpec((d, n), lambda i: (0, 0), pipeline_mode=once),
            pl.BlockSpec((d, LORA_PAD), lambda i: (0, 0), pipeline_mode=once),
            pl.BlockSpec((1, n - n_attn_tiles * tn), lambda i: (0, 0)),
            pl.BlockSpec((1, LORA_PAD), lambda i: (0, 0)),
        ],
        out_specs=[
            pl.BlockSpec((tm, n), lambda i: (i, 0)),
            pl.BlockSpec((tm, LORA_PAD), lambda i: (i, 0)),
        ],
        out_shape=[
            jax.ShapeDtypeStruct((m, n), F32),
            jax.ShapeDtypeStruct((m, LORA_PAD), F32),
        ],
        scratch_shapes=[
            pltpu.VMEM((6 - n_attn_tiles, 8, tn), F32),
            pltpu.VMEM((8, LORA_PAD), F32),
        ],
        compiler_params=_cparams(("arbitrary",)),
        name="inproj",
    )(x2, posb, sc1, sh1, g1, w_main, w_lora, mu_main, mu_lora)


def _attn_kernel(q_ref, kc_ref, kp_ref, vc_ref, vp_ref, o_ref, pv_scr, m_scr, l_scr):
    blk = WINDOW_BLOCK
    has_prev = pl.program_id(2) > 0
    lane = lax.broadcasted_iota(jnp.int32, (1, LANES), 1)
    head0 = lane < HEAD_DIM
    qi = lax.broadcasted_iota(jnp.int32, (blk, 2 * blk), 0)
    kj = lax.broadcasted_iota(jnp.int32, (blk, 2 * blk), 1)
    in_cur = kj >= blk
    slack = jnp.where(in_cur, qi - (kj - blk), kj - qi)
    bias_full = jnp.where(slack >= 0, 0.0, NEG)
    bias_cur = jnp.where(in_cur, bias_full, NEG)
    bias_edge = jnp.where(has_prev, bias_full, bias_cur)
    cur = (kc_ref, vc_ref)
    prv = (kp_ref, vp_ref)

    def run_blocks(d, specs, emit):
        rows = lambda start: pl.ds(start, blk, stride=d) if d > 1 else pl.ds(start, blk)
        qs = [q_ref[rows(q0), :] * (HEAD_DIM ** -0.5) for q0, _, _, _ in specs]
        kw = [jnp.concatenate([pr[0][rows(p0), :], kc_ref[rows(q0), :]], axis=0).astype(BF16)
              for q0, pr, p0, _ in specs]
        vw = [jnp.concatenate([pr[1][rows(p0), :], vc_ref[rows(q0), :]], axis=0).astype(BF16)
              for q0, pr, p0, _ in specs]
        qh = [jnp.where(mh, q, 0.0).astype(BF16) for q in qs
              for mh in (head0, jnp.logical_not(head0))]
        two = lambda xs: [x for x in xs for _ in range(2)]
        s = [lax.dot_general(q, k, (((1,), (1,)), ((), ())), preferred_element_type=F32) + b
             for q, k, b in zip(qh, two(kw), two([sp[3] for sp in specs]))]
        mx = [jnp.max(x, axis=-1, keepdims=True) for x in s]
        p = [jnp.exp(x - m_) for x, m_ in zip(s, mx)]
        den = [jnp.sum(x, axis=-1, keepdims=True) for x in p]
        pv = [jnp.dot(x.astype(BF16), v, preferred_element_type=F32) for x, v in zip(p, two(vw))]
        for n, (q0, _, _, _) in enumerate(specs):
            emit(rows(q0), jnp.where(head0, pv[2 * n], pv[2 * n + 1]),
                 jnp.where(head0, mx[2 * n], mx[2 * n + 1]),
                 jnp.where(head0, den[2 * n], den[2 * n + 1]))

    def groups(blocks):
        return [blocks[n:n + ATTN_GROUP] for n in range(0, len(blocks), ATTN_GROUP)]

    for slot, d in ((0, DILATIONS[1]), (1, DILATIONS[2])):
        unit = blk * d
        n_units = SPAN // unit

        def park(rr, pv, mx, den, slot=slot):
            pv_scr[slot, rr, :] = pv
            m_scr[slot, rr, :] = mx
            l_scr[slot, rr, :] = den

        blocks = [(r, prv, (n_units - 1) * unit + r, bias_edge) for r in range(d)]
        blocks += [(n * unit + r, cur, (n - 1) * unit + r, bias_full)
                   for n in range(1, n_units) for r in range(d)]
        for grp in groups(blocks):
            run_blocks(d, grp, park)

    def finish(rr, pv, mx, den):
        m4, m16 = m_scr[0, rr, :], m_scr[1, rr, :]
        mm = jnp.maximum(mx, jnp.maximum(m4, m16))
        e1, e4, e16 = jnp.exp(mx - mm), jnp.exp(m4 - mm), jnp.exp(m16 - mm)
        num = e1 * pv + e4 * pv_scr[0, rr, :] + e16 * pv_scr[1, rr, :]
        dn = e1 * den + e4 * l_scr[0, rr, :] + e16 * l_scr[1, rr, :]
        o_ref[rr, :] = (num / dn).astype(o_ref.dtype)

    blocks = [(0, prv, SPAN - blk, bias_edge)]
    blocks += [(n * blk, cur, (n - 1) * blk, bias_full) for n in range(1, SPAN // blk)]
    for grp in groups(blocks):
        run_blocks(1, grp, finish)


def _attention(p3, *, n_pairs):
    b, s, _ = p3.shape
    n_spans = s // SPAN

    def spec(col0, prev):
        if prev:
            return pl.BlockSpec((None, SPAN, LANES),
                                lambda bi, hp, si: (bi, jnp.maximum(si - 1, 0), col0 + hp))
        return pl.BlockSpec((None, SPAN, LANES), lambda bi, hp, si: (bi, si, col0 + hp))

    return pl.pallas_call(
        _attn_kernel,
        grid=(b, n_pairs, n_spans),
        in_specs=[spec(0, False), spec(n_pairs, False), spec(n_pairs, True),
                  spec(2 * n_pairs, False), spec(2 * n_pairs, True)],
        out_specs=pl.BlockSpec((None, SPAN, LANES), lambda bi, hp, si: (bi, si, hp)),
        out_shape=jax.ShapeDtypeStruct((b, s, n_pairs * LANES), BF16),
        scratch_shapes=[pltpu.VMEM((2, SPAN, LANES), F32)] * 3,
        compiler_params=_cparams(("arbitrary", "arbitrary", "arbitrary")),
        name="attn",
    )(p3, p3, p3, p3, p3)


def _unit_lower_inverse(a, masks, bd):
    eye, diag4, offs = masks
    mm = lambda xs, ys: [jnp.dot(x.astype(BF16), bd(y).astype(BF16), preferred_element_type=F32)
                         for x, y in zip(xs, ys)]
    ad = [jnp.where(diag4, x, 0.0) for x in a]
    a2 = mm(ad, ad)
    a3 = mm(a2, ad)
    t = [eye + x1 + x2 + x3 for x1, x2, x3 in zip(ad, a2, a3)]
    yield
    for off in offs:
        inner = mm([jnp.where(off, x, 0.0) for x in a], t)
        t = [x + y for x, y in zip(t, mm(t, inner))]
        yield
    return t


def _rwkv_kernel(r_ref, k_ref, v_ref, l_ref, wd_ref, wa_ref, wg_ref, cp_ref, o_ref,
                 st_scr, rst_scr, ov_scr, tr_scr, inj_scr, dec_scr, bon_scr, g_scr, *, group):
    n_streams = r_ref.shape[0]
    ti = pl.program_id(1)
    n_tiles = pl.num_programs(1) - 1
    slot_new = ti % 2
    slot_old = 1 - slot_new

    @pl.when(ti == 0)
    def _():
        st_scr[...] = jnp.zeros_like(st_scr)

    c = CHUNK
    lane = lax.broadcasted_iota(jnp.int32, (1, LANES), 1)
    head0 = lane < HEAD_DIM

    def head_sum(x):
        s0 = jnp.sum(jnp.where(head0, x, 0.0), axis=-1, keepdims=True)
        s1 = jnp.sum(jnp.where(head0, 0.0, x), axis=-1, keepdims=True)
        return jnp.where(head0, s0, s1)

    def bd(x):
        wide = x.shape[1] // LANES
        m0 = head0 if wide == 1 else jnp.concatenate([head0] * wide, axis=1)
        return jnp.concatenate([jnp.where(m0, x, 0.0), jnp.where(m0, 0.0, x)], axis=0)

    def bdot(x, y):
        return jnp.dot(x.astype(BF16), y.astype(BF16), preferred_element_type=F32)

    def bdot_t(x, y):
        return lax.dot_general(x.astype(BF16), y.astype(BF16), (((1,), (1,)), ((), ())),
                               preferred_element_type=F32)

    def tdot(x, y):
        return lax.dot_general(x.astype(BF16), y.astype(BF16), (((0,), (0,)), ((), ())),
                               preferred_element_type=F32)

    def split(x):
        hi = x.astype(BF16)
        return hi, (x - hi.astype(F32)).astype(BF16)

    chunks = lambda x: [x[n * c:(n + 1) * c] for n in range(group)]
    each = lambda f, *xs: [f(*args) for args in zip(*xs)]

    def chunk_terms():
        ri = lax.broadcasted_iota(jnp.int32, (c, 2 * c), 0)
        cj = lax.broadcasted_iota(jnp.int32, (c, 2 * c), 1) % c
        same = lambda size: (ri // size) == (cj // size)
        strict = cj < ri
        incl = cj <= ri
        eye = (ri == cj).astype(F32)
        sizes = []
        size = 4
        while size < c:
            sizes.append(size)
            size *= 2
        inv_masks = (eye, same(4), [same(2 * sz) & jnp.logical_not(same(sz)) for sz in sizes])
        tri = (lax.broadcasted_iota(jnp.int32, (c, c), 1)
               <= lax.broadcasted_iota(jnp.int32, (c, c), 0)).astype(BF16)
        r2 = lax.broadcasted_iota(jnp.int32, (LANES, LANES), 0)
        c2 = lax.broadcasted_iota(jnp.int32, (LANES, LANES), 1)
        same_head = (r2 // HEAD_DIM) == (c2 // HEAD_DIM)
        cp = cp_ref[...]
        w0, a0, k_k, k_a, r_k = (cp[n:n + 1, :] for n in range(5))
        wd = wd_ref[...]
        wa = wa_ref[...]
        wg = wg_ref[...]

        q = {}
        for bi in range(n_streams):
            r = r_ref[bi]
            k = k_ref[bi]
            v = v_ref[bi]
            th = l_ref[bi, :, 0:LANES]
            al = l_ref[bi, :, LANES:2 * LANES]
            sg = l_ref[bi, :, 2 * LANES:]
            z = w0 + bdot(th, wd)
            w = jnp.minimum(z, 0.0) - jnp.log(1.0 + jnp.exp(-jnp.abs(z))) - 0.5
            ld = -jnp.exp(w)
            a = _sigmoid(a0 + bdot(al, wa))
            g = bdot(sg, wg)
            kk = k * k_k
            kk = kk / jnp.maximum(jnp.sqrt(head_sum(kk * kk)), 1e-12)
            kp = k * (1.0 + (a - 1.0) * k_a)
            av = -kk
            bv = kk * a
            bonus = head_sum(r * kp * r_k) * v
            ld_hi, ld_lo = split(ld)
            lc = jnp.concatenate(
                each(lambda hi, lo: (jnp.dot(tri, hi, preferred_element_type=F32)
                                     + jnp.dot(tri, lo, preferred_element_type=F32)),
                     chunks(ld_hi), chunks(ld_lo)), axis=0)
            ltot = [x[c - 1:c, :] for x in chunks(lc)]
            ltot_rows = jnp.concatenate([jnp.broadcast_to(x, (c, LANES)) for x in ltot], axis=0)
            e_nl = jnp.exp(-lc)
            to_end = jnp.exp(ltot_rows - lc)
            vals = dict(rt=chunks(r * jnp.exp(lc)), kt=chunks(kp * e_nl), bt=chunks(bv * e_nl),
                        at=chunks(av * jnp.exp(lc - ld)), kh=chunks(kp * to_end),
                        bh=chunks(bv * to_end), v=chunks(v))
            for name, val in vals.items():
                q.setdefault(name, []).extend(val)
            for n in range(group):
                i = bi * group + n
                dec_scr[slot_new, i] = jnp.broadcast_to(jnp.exp(ltot[n]), (8, LANES))
                bon_scr[slot_new, i] = bonus[n * c:(n + 1) * c]
                g_scr[slot_new, i] = g[n * c:(n + 1) * c]
        yield
        aa = each(lambda a_, r_, b_, k_: bdot_t(jnp.concatenate([a_, r_], axis=0),
                                                jnp.concatenate([bd(b_), bd(k_)], axis=0)),
                  q["at"], q["rt"], q["bt"], q["kt"])
        a_ab = [jnp.where(strict, x[:c, :LANES], 0.0) for x in aa]
        a_ak = [jnp.where(strict, x[:c, LANES:], 0.0) for x in aa]
        a_rb = [jnp.where(incl, x[c:, :LANES], 0.0) for x in aa]
        a_rk = [jnp.where(incl, x[c:, LANES:], 0.0) for x in aa]
        yield
        av = each(lambda a_, v_: bdot(a_, bd(v_)), a_ak, q["v"])
        t_inv = yield from _unit_lower_inverse(a_ab, inv_masks, bd)
        y = each(lambda t_, a_, av_: bdot(t_, bd(jnp.concatenate([a_, av_], axis=1))),
                 t_inv, q["at"], av)
        a_st = [x[:, :LANES] for x in y]
        u_v = [x[:, LANES:] for x in y]
        yield
        zz = each(lambda rb_, rk_, y_, v_: bdot(
            jnp.concatenate([rb_, rk_], axis=1),
            jnp.concatenate([bd(y_), bd(jnp.concatenate([jnp.zeros_like(v_), v_], axis=1))], axis=0)),
            a_rb, a_rk, y, q["v"])
        for i, (rt_, z_) in enumerate(zip(q["rt"], zz)):
            rst_scr[slot_new, i] = rt_ + z_[:, :LANES]
            ov_scr[slot_new, i] = z_[:, LANES:]
        yield
        trans = each(lambda a_, b_: jnp.where(same_head, tdot(a_, b_), 0.0), a_st, q["bh"])
        for i, t_ in enumerate(trans):
            tr_scr[slot_new, i] = t_.astype(BF16)

        def injected(u_, v_, b_, k_):
            full = tdot(jnp.concatenate([u_, v_], axis=0), jnp.concatenate([b_, k_], axis=0))
            return jnp.where(head0, full[:HEAD_DIM], full[HEAD_DIM:])
        for i, x in enumerate(each(injected, u_v, q["v"], q["bh"], q["kh"])):
            inj_scr[slot_new, i] = x

    def recurrence_steps():
        cp = cp_ref[...]
        gn_g, gn_b = cp[5:6, :], cp[6:7, :]
        state = {}

        def step(n):
            if n == 0:
                state["st"] = [st_scr[bi] for bi in range(n_streams)]
            st = state["st"]
            idx = [bi * group + n for bi in range(n_streams)]
            o = [bdot_t(rst_scr[slot_old, i], bd(s)) + ov_scr[slot_old, i] for i, s in zip(idx, st)]
            parts = [split(s) for s in st]
            tb = [tr_scr[slot_old, i] for i in idx]
            hi = [jnp.dot(p_[0], t_, preferred_element_type=F32) for p_, t_ in zip(parts, tb)]
            lo = [jnp.dot(p_[1], t_, preferred_element_type=F32) for p_, t_ in zip(parts, tb)]
            st = [s * dec_scr[slot_old, i, 0:1, :] + h_ + l_ + inj_scr[slot_old, i]
                  for s, i, h_, l_ in zip(st, idx, hi, lo)]
            state["st"] = st
            for bi, i in enumerate(idx):
                mean = head_sum(o[bi]) * (1.0 / HEAD_DIM)
                dev = o[bi] - mean
                var = head_sum(dev * dev) * (1.0 / HEAD_DIM)
                on = dev * lax.rsqrt(var + GN_EPS) * gn_g + gn_b
                o_ref[bi, n * c:(n + 1) * c, :] = (
                    (on + bon_scr[slot_old, i]) * g_scr[slot_old, i]).astype(o_ref.dtype)
            if n == group - 1:
                for bi in range(n_streams):
                    st_scr[bi] = st[bi]
        return [functools.partial(step, n) for n in range(group)]

    def run(with_terms, with_recurrence):
        steps = recurrence_steps() if with_recurrence else []
        if with_terms:
            for _ in chunk_terms():
                if steps:
                    steps.pop(0)()
        for s in steps:
            s()

    @pl.when(ti == 0)
    def _():
        run(True, False)

    @pl.when((ti > 0) & (ti < n_tiles))
    def _():
        run(True, True)

    @pl.when(ti == n_tiles)
    def _():
        run(False, True)


def _rwkv(p3, l3, wd, wa, wg, cp, *, n_pairs, group):
    b, s, _ = p3.shape
    col0 = 3 * n_pairs
    tt = group * CHUNK
    n_tiles = s // tt
    n_terms = b * group
    tile_in = lambda ti: jnp.minimum(ti, n_tiles - 1)

    def pspec(off):
        return pl.BlockSpec((b, tt, LANES), lambda hp, ti: (0, tile_in(ti), col0 + off + hp))

    wspec = lambda rows: pl.BlockSpec((rows, LANES), lambda hp, ti: (0, hp))
    terms = lambda rows, dt: pltpu.VMEM((2, n_terms, rows, LANES), dt)
    kern = functools.partial(_rwkv_kernel, group=group)
    return pl.pallas_call(
        kern,
        grid=(n_pairs, n_tiles + 1),
        in_specs=[pspec(0), pspec(n_pairs), pspec(2 * n_pairs),
                  pl.BlockSpec((b, tt, LORA_PAD), lambda hp, ti: (0, tile_in(ti), 0)),
                  wspec(LANES), wspec(LANES), wspec(2 * LANES), wspec(8)],
        out_specs=pl.BlockSpec((b, tt, LANES), lambda hp, ti: (0, jnp.maximum(ti - 1, 0), hp)),
        out_shape=jax.ShapeDtypeStruct((b, s, n_pairs * LANES), BF16),
        scratch_shapes=[pltpu.VMEM((b, HEAD_DIM, LANES), F32),
                        terms(CHUNK, F32), terms(CHUNK, F32), terms(LANES, BF16),
                        terms(HEAD_DIM, F32), terms(8, F32), terms(CHUNK, F32), terms(CHUNK, F32)],
        compiler_params=_cparams(("arbitrary", "arbitrary")),
        name="rwkv",
    )(p3, p3, p3, l3, wd, wa, wg, cp)


def _mixffn_kernel(a_ref, r_ref, wa_ref, wr_ref, x_ref, gt1_ref, sc_ref, sh_ref, gt2_ref, g2_ref,
                   gf_ref, wg_ref, wu_ref, wd_ref, o_ref, x1_scr, h_scr, acc_scr):
    f = pl.program_id(1)

    def rms(x, g):
        return x * lax.rsqrt(jnp.mean(x * x, axis=-1, keepdims=True) + RMS_EPS) * g

    @pl.when(f == 0)
    def _():
        mix = jnp.dot(a_ref[...], wa_ref[...], preferred_element_type=F32)
        mix += jnp.dot(r_ref[...], wr_ref[...], preferred_element_type=F32)
        x1 = x_ref[...] + gt1_ref[...] * mix
        x1_scr[...] = x1
        h = rms(x1, g2_ref[...]) * (1.0 + sc_ref[...]) + sh_ref[...]
        h_scr[...] = h.astype(BF16)
        acc_scr[...] = jnp.zeros_like(acc_scr)

    h = h_scr[...]
    gate = jnp.dot(h, wg_ref[...], preferred_element_type=F32)
    up = jnp.dot(h, wu_ref[...], preferred_element_type=F32)
    act = (gate * _sigmoid(gate) * up).astype(BF16)
    acc_scr[...] += jnp.dot(act, wd_ref[...], preferred_element_type=F32)

    @pl.when(f == pl.num_programs(1) - 1)
    def _():
        y = x1_scr[...] + gt2_ref[...] * acc_scr[...]
        o_ref[...] = rms(y, gf_ref[...])


def _mixffn(attn, rw, w_a, w_r, x2, gt1, sc2, sh2, gt2, g2, gf, w_gate, w_up, w_down,
            *, seq, tm, tf):
    m, d = x2.shape
    half = attn.shape[1]
    fdim = w_gate.shape[1]
    tiles_per_batch = seq // tm
    once = pl.Buffered(1)
    vec = pl.BlockSpec((None, 1, d), lambda i, f: (i // tiles_per_batch, 0, 0))
    par = pl.BlockSpec((1, d), lambda i, f: (0, 0))
    return pl.pallas_call(
        _mixffn_kernel,
        grid=(m // tm, fdim // tf),
        in_specs=[
            pl.BlockSpec((tm, half), lambda i, f: (i, 0)),
            pl.BlockSpec((tm, half), lambda i, f: (i, 0)),
            pl.BlockSpec((half, d), lambda i, f: (0, 0), pipeline_mode=once),
            pl.BlockSpec((half, d), lambda i, f: (0, 0), pipeline_mode=once),
            pl.BlockSpec((tm, d), lambda i, f: (i, 0)),
            vec, vec, vec, vec, par, par,
            pl.BlockSpec((d, tf), lambda i, f: (0, f)),
            pl.BlockSpec((d, tf), lambda i, f: (0, f)),
            pl.BlockSpec((tf, d), lambda i, f: (f, 0)),
        ],
        out_specs=pl.BlockSpec((tm, d), lambda i, f: (i, 0)),
        out_shape=jax.ShapeDtypeStruct((m, d), F32),
        scratch_shapes=[pltpu.VMEM((tm, d), F32), pltpu.VMEM((tm, d), BF16),
                        pltpu.VMEM((tm, d), F32)],
        compiler_params=_cparams(("arbitrary", "arbitrary"), MIXFFN_VMEM_LIMIT),
        name="mixffn",
    )(attn, rw, w_a, w_r, x2, gt1, sc2, sh2, gt2, g2, gf, w_gate, w_up, w_down)


def _largest_tile(total, cap, step):
    t = min(total, cap)
    t -= t % step
    while total % t:
        t -= step
    return t


def kernel(x, c, positions, w_ada, b_ada, norm1_g, norm2_g, normf_g, w_in, w_out, mu_shift, w0,
           w_decay_up, a0, w_iclr_up, w_gate_up, k_k, k_a, r_k, gn_g, gn_b, w_ffn_gate,
           w_ffn_up, w_ffn_down):
    b, s, d = x.shape
    depth = w_ada.shape[0]
    assert depth == 1, "the final norm is fused into the (single) channel mixer"
    half = d // 2
    n_pairs = half // LANES
    m = b * s
    assert s % SPAN == 0 and half % LANES == 0
    tm = _largest_tile(s, 512, 8)
    assert s % (RWKV_GROUP * CHUNK) == 0
    fdim = w_ffn_gate.shape[-1]
    tf = _largest_tile(fdim, 512, LANES)

    posb = jnp.broadcast_to(positions.astype(F32).reshape(m, 1), (m, LANES))
    x2 = x.reshape(m, d)
    c8 = jnp.pad(c, ((0, 8 - b % 8 if b % 8 else 0), (0, 0)))

    def lora_cols(t, axis):
        parts = jnp.split(t, [DECAY_LORA, DECAY_LORA + ICLR_LORA], axis=axis)
        offs = (0, LANES, 2 * LANES)
        shape = list(t.shape)
        shape[axis] = LORA_PAD
        out = jnp.zeros(shape, t.dtype)
        for part, off in zip(parts, offs):
            idx = [slice(None)] * t.ndim
            idx[axis] = slice(off, off + part.shape[axis])
            out = out.at[tuple(idx)].set(part)
        return out

    def pad_rows(t, rows):
        return jnp.pad(t, ((0, rows - t.shape[0]), (0, 0)))

    for i in range(depth):
        ada = _ada(c8, w_ada[i], b_ada[i][None, :])[:b]
        sh1, sc1, gt1, sh2, sc2, gt2 = (t[:, None, :] for t in jnp.split(ada, 6, axis=-1))

        w_main = w_in[i][:, :6 * half].astype(BF16)
        w_lora = lora_cols(w_in[i][:, 6 * half:], 1).astype(BF16)
        mu_main = mu_shift[i][None, :3 * half]
        mu_lora = lora_cols(mu_shift[i][None, 3 * half:], 1)
        p2, l2 = _inproj(x2, posb, sc1, sh1, norm1_g[i][None, :], w_main, w_lora,
                         mu_main, mu_lora, seq=s, tm=_largest_tile(s, 256, 8))
        p3 = p2.reshape(b, s, 6 * half)
        l3 = l2.reshape(b, s, LORA_PAD)

        attn = _attention(p3, n_pairs=n_pairs)

        cp = jnp.stack([w0[i], a0[i], k_k[i], k_a[i], r_k[i].reshape(-1), gn_g[i], gn_b[i],
                        jnp.zeros_like(w0[i])])
        rw = _rwkv(p3, l3, pad_rows(w_decay_up[i], LANES), pad_rows(w_iclr_up[i], LANES),
                   pad_rows(w_gate_up[i], 2 * LANES), cp, n_pairs=n_pairs, group=RWKV_GROUP)

        wo = w_out[i].astype(BF16)
        x2 = _mixffn(attn.reshape(m, half), rw.reshape(m, half), wo[:half], wo[half:], x2, gt1,
                     sc2, sh2, gt2, norm2_g[i][None, :], normf_g[None, :],
                     w_ffn_gate[i].astype(BF16), w_ffn_up[i].astype(BF16),
                     w_ffn_down[i].astype(BF16), seq=s, tm=tm, tf=tf)
    return x2.reshape(b, s, d)
```

```python
import functools
import math

import jax
import jax.numpy as jnp
from jax import lax
from jax.experimental import pallas as pl
from jax.experimental.pallas import tpu as pltpu

HEAD_DIM = 64
LANES = 128
ROT_DIM = HEAD_DIM // 4
ROPE_THETA = 500000.0
RMS_EPS = 1e-6
GN_EPS = HEAD_DIM * 1e-5
DILATIONS = (1, 4, 16)
WINDOW_BLOCK = 128
SPAN = WINDOW_BLOCK * max(DILATIONS)
DECAY_LORA = 64
ICLR_LORA = 64
GATE_LORA = 160
LORA_PAD = 512
CHUNK = 64
ATTN_GROUP = 8
RWKV_GROUP = 8
NEG = -1e30
VMEM_LIMIT = 56 * 1024 * 1024
MIXFFN_VMEM_LIMIT = 60 * 1024 * 1024

F32 = jnp.float32
BF16 = jnp.bfloat16


def _cparams(sem, vmem_limit=VMEM_LIMIT):
    return pltpu.CompilerParams(dimension_semantics=sem, vmem_limit_bytes=vmem_limit)


def _sigmoid(z):
    return 1.0 / (1.0 + jnp.exp(-z))


def _ada_kernel(ct_ref, w_ref, b_ref, o_ref):
    ct = ct_ref[...]
    s = ct * _sigmoid(ct)
    w = w_ref[...]
    for row in range(o_ref.shape[0]):
        o_ref[row:row + 1, :] = (jnp.sum(s[:, row:row + 1] * w, axis=0, keepdims=True)
                                 + b_ref[...])


def _ada(c, w_ada, b_ada):
    rows, d = c.shape
    assert rows <= 8
    n = w_ada.shape[1]
    tn = _largest_tile(n, 1024, LANES)
    ct = jnp.pad(c, ((0, 8 - rows), (0, 0))).T
    return pl.pallas_call(
        _ada_kernel,
        grid=(n // tn,),
        in_specs=[
            pl.BlockSpec((d, 8), lambda j: (0, 0)),
            pl.BlockSpec((d, tn), lambda j: (0, j)),
            pl.BlockSpec((1, tn), lambda j: (0, j)),
        ],
        out_specs=pl.BlockSpec((rows, tn), lambda j: (0, j)),
        out_shape=jax.ShapeDtypeStruct((rows, n), F32),
        compiler_params=_cparams(("arbitrary",)),
        name="ada",
    )(ct, w_ada, b_ada)


def _inproj_kernel(x_ref, pos_ref, sc_ref, sh_ref, g_ref, w_ref, wl_ref, mu_ref, mul_ref,
                   p_ref, l_ref, last_scr, lastl_scr, *, tiles_per_batch, n_attn_tiles):
    i = pl.program_id(0)
    tm = x_ref.shape[0]
    n_tiles = last_scr.shape[0] + n_attn_tiles
    tn = p_ref.shape[1] // n_tiles
    first_in_batch = (i % tiles_per_batch) == 0

    x = x_ref[...]
    ms = jnp.mean(x * x, axis=-1, keepdims=True)
    y = x * lax.rsqrt(ms + RMS_EPS) * g_ref[...]
    h = (y * (1.0 + sc_ref[...]) + sh_ref[...]).astype(BF16)
    lane = lax.broadcasted_iota(jnp.int32, (1, LANES), 1)
    freq = (lane % (ROT_DIM // 2)).astype(F32)
    inv = jnp.exp(freq * (-2.0 / ROT_DIM * math.log(ROPE_THETA)))
    ang = pos_ref[...] * inv
    cs = jnp.cos(ang)
    sn = jnp.sin(ang)
    lm = lane % HEAD_DIM
    ct = jnp.where(lm < ROT_DIM, cs, 1.0)
    sa = jnp.where(lm < ROT_DIM // 2, -sn, 0.0)
    sb = jnp.where((lm >= ROT_DIM // 2) & (lm < ROT_DIM), sn, 0.0)

    def shifted(val, carry_row, mu):
        prev = pltpu.roll(val, 1, 0)
        row = lax.broadcasted_iota(jnp.int32, val.shape, 0)
        carry = jnp.where(first_in_batch, 0.0, carry_row)
        prev = jnp.where(row == 0, carry, prev)
        return val + (prev - val) * mu

    for jt in range(n_tiles):
        res = jnp.dot(h, w_ref[:, jt * tn:(jt + 1) * tn], preferred_element_type=F32)
        if jt < 2:
            for gidx in range(tn // LANES):
                t = res[:, gidx * LANES:(gidx + 1) * LANES]
                up = pltpu.roll(t, LANES - ROT_DIM // 2, 1)
                dn = pltpu.roll(t, ROT_DIM // 2, 1)
                c0 = jt * tn + gidx * LANES
                p_ref[:, c0:c0 + LANES] = t * ct + up * sa + dn * sb
        elif jt < n_attn_tiles:
            p_ref[:, jt * tn:(jt + 1) * tn] = res
        else:
            jj = jt - n_attn_tiles
            p_ref[:, jt * tn:(jt + 1) * tn] = shifted(res, last_scr[jj, 7:8, :],
                                                     mu_ref[:, jj * tn:(jj + 1) * tn])
            last_scr[jj] = res[tm - 8:, :]

    resl = jnp.dot(h, wl_ref[...], preferred_element_type=F32)
    yl = shifted(resl, lastl_scr[7:8, :], mul_ref[...])
    lastl_scr[...] = resl[tm - 8:, :]
    l_ref[:, 0:LANES] = jnp.tanh(yl[:, 0:LANES])
    l_ref[:, LANES:2 * LANES] = yl[:, LANES:2 * LANES]
    l_ref[:, 2 * LANES:] = _sigmoid(yl[:, 2 * LANES:])


def _inproj(x2, posb, sc1, sh1, g1, w_all, w_lora, mu_main, mu_lora, *, seq, tm):
    m, d = x2.shape
    n = 2 * mu_main.shape[1]
    tn = n // 6
    n_attn_tiles = 3
    tiles_per_batch = seq // tm
    kern = functools.partial(_inproj_kernel, tiles_per_batch=tiles_per_batch,
                             n_attn_tiles=n_attn_tiles)
    once = pl.Buffered(1)
    return pl.pallas_call(
        kern,
        grid=(m // tm,),
        in_specs=[
            pl.BlockSpec((tm, d), lambda i: (i, 0)),
            pl.BlockSpec((tm, LANES), lambda i: (i, 0)),
            pl.BlockSpec((None, 1, d), lambda i: (i // tiles_per_batch, 0, 0)),
            pl.BlockSpec((None, 1, d), lambda i: (i // tiles_per_batch, 0, 0)),
            pl.BlockSpec((1, d), lambda i: (0, 0)),
            pl.BlockSpec((d, n), lambda i: (0, 0), pipeline_mode=once),
            pl.BlockSpec((d, LORA_PAD), lambda i: (0, 0), pipeline_mode=once),
            pl.BlockSpec((1, n - n_attn_tiles * tn), lambda i: (0, 0)),
            pl.BlockSpec((1, LORA_PAD), lambda i: (0, 0)),
        ],
        out_specs=[
            pl.BlockSpec((tm, n), lambda i: (i, 0)),
            pl.BlockSpec((tm, LORA_PAD), lambda i: (i, 0)),
        ],
        out_shape=[
            jax.ShapeDtypeStruct((m, n), F32),
            jax.ShapeDtypeStruct((m, LORA_PAD), F32),
        ],
        scratch_shapes=[
            pltpu.VMEM((6 - n_attn_tiles, 8, tn), F32),
            pltpu.VMEM((8, LORA_PAD), F32),
        ],
        compiler_params=_cparams(("arbitrary",)),
        name="inproj",
    )(x2, posb, sc1, sh1, g1, w_all, w_lora, mu_main, mu_lora)


def _attn_kernel(q_ref, kc_ref, kp_ref, vc_ref, vp_ref, o_ref, pv_scr, m_scr, l_scr):
    blk = WINDOW_BLOCK
    has_prev = pl.program_id(2) > 0
    lane = lax.broadcasted_iota(jnp.int32, (1, LANES), 1)
    head0 = lane < HEAD_DIM
    qi = lax.broadcasted_iota(jnp.int32, (blk, 2 * blk), 0)
    kj = lax.broadcasted_iota(jnp.int32, (blk, 2 * blk), 1)
    in_cur = kj >= blk
    slack = jnp.where(in_cur, qi - (kj - blk), kj - qi)
    bias_full = jnp.where(slack >= 0, 0.0, NEG)
    bias_cur = jnp.where(in_cur, bias_full, NEG)
    bias_edge = jnp.where(has_prev, bias_full, bias_cur)
    cur = (kc_ref, vc_ref)
    prv = (kp_ref, vp_ref)

    def run_blocks(d, specs, emit):
        rows = lambda start: pl.ds(start, blk, stride=d) if d > 1 else pl.ds(start, blk)
        qs = [q_ref[rows(q0), :] * (HEAD_DIM ** -0.5) for q0, _, _, _ in specs]
        kw = [jnp.concatenate([pr[0][rows(p0), :], kc_ref[rows(q0), :]], axis=0).astype(BF16)
              for q0, pr, p0, _ in specs]
        vw = [jnp.concatenate([pr[1][rows(p0), :], vc_ref[rows(q0), :]], axis=0).astype(BF16)
              for q0, pr, p0, _ in specs]
        qh = [jnp.where(mh, q, 0.0).astype(BF16) for q in qs
              for mh in (head0, jnp.logical_not(head0))]
        two = lambda xs: [x for x in xs for _ in range(2)]
        s = [lax.dot_general(q, k, (((1,), (1,)), ((), ())), preferred_element_type=F32) + b
             for q, k, b in zip(qh, two(kw), two([sp[3] for sp in specs]))]
        mx = [jnp.max(x, axis=-1, keepdims=True) for x in s]
        p = [jnp.exp(x - m_) for x, m_ in zip(s, mx)]
        den = [jnp.sum(x, axis=-1, keepdims=True) for x in p]
        pv = [jnp.dot(x.astype(BF16), v, preferred_element_type=F32) for x, v in zip(p, two(vw))]
        for n, (q0, _, _, _) in enumerate(specs):
            emit(rows(q0), jnp.where(head0, pv[2 * n], pv[2 * n + 1]),
                 jnp.where(head0, mx[2 * n], mx[2 * n + 1]),
                 jnp.where(head0, den[2 * n], den[2 * n + 1]))

    def groups(blocks):
        return [blocks[n:n + ATTN_GROUP] for n in range(0, len(blocks), ATTN_GROUP)]

    for slot, d in ((0, DILATIONS[1]), (1, DILATIONS[2])):
        unit = blk * d
        n_units = SPAN // unit

        def park(rr, pv, mx, den, slot=slot):
            pv_scr[slot, rr, :] = pv
            m_scr[slot, rr, :] = mx
            l_scr[slot, rr, :] = den

        blocks = [(r, prv, (n_units - 1) * unit + r, bias_edge) for r in range(d)]
        blocks += [(n * unit + r, cur, (n - 1) * unit + r, bias_full)
                   for n in range(1, n_units) for r in range(d)]
        for grp in groups(blocks):
            run_blocks(d, grp, park)

    def finish(rr, pv, mx, den):
        m4, m16 = m_scr[0, rr, :], m_scr[1, rr, :]
        mm = jnp.maximum(mx, jnp.maximum(m4, m16))
        e1, e4, e16 = jnp.exp(mx - mm), jnp.exp(m4 - mm), jnp.exp(m16 - mm)
        num = e1 * pv + e4 * pv_scr[0, rr, :] + e16 * pv_scr[1, rr, :]
        dn = e1 * den + e4 * l_scr[0, rr, :] + e16 * l_scr[1, rr, :]
        o_ref[rr, :] = (num / dn).astype(o_ref.dtype)

    blocks = [(0, prv, SPAN - blk, bias_edge)]
    blocks += [(n * blk, cur, (n - 1) * blk, bias_full) for n in range(1, SPAN // blk)]
    for grp in groups(blocks):
        run_blocks(1, grp, finish)


def _attention(p3, *, n_pairs):
    b, s, _ = p3.shape
    n_spans = s // SPAN

    def spec(col0, prev):
        if prev:
            return pl.BlockSpec((None, SPAN, LANES),
                                lambda bi, hp, si: (bi, jnp.maximum(si - 1, 0), col0 + hp))
        return pl.BlockSpec((None, SPAN, LANES), lambda bi, hp, si: (bi, si, col0 + hp))

    return pl.pallas_call(
        _attn_kernel,
        grid=(b, n_pairs, n_spans),
        in_specs=[spec(0, False), spec(n_pairs, False), spec(n_pairs, True),
                  spec(2 * n_pairs, False), spec(2 * n_pairs, True)],
        out_specs=pl.BlockSpec((None, SPAN, LANES), lambda bi, hp, si: (bi, si, hp)),
        out_shape=jax.ShapeDtypeStruct((b, s, n_pairs * LANES), BF16),
        scratch_shapes=[pltpu.VMEM((2, SPAN, LANES), F32)] * 3,
        compiler_params=_cparams(("arbitrary", "arbitrary", "arbitrary")),
        name="attn",
    )(p3, p3, p3, p3, p3)


def _unit_lower_inverse(a, masks, bd):
    eye, diag4, offs = masks
    mm = lambda xs, ys: [jnp.dot(x.astype(BF16), bd(y).astype(BF16), preferred_element_type=F32)
                         for x, y in zip(xs, ys)]
    ad = [jnp.where(diag4, x, 0.0) for x in a]
    a2 = mm(ad, ad)
    a3 = mm(a2, ad)
    t = [eye + x1 + x2 + x3 for x1, x2, x3 in zip(ad, a2, a3)]
    yield
    for off in offs:
        inner = mm([jnp.where(off, x, 0.0) for x in a], t)
        t = [x + y for x, y in zip(t, mm(t, inner))]
        yield
    return t


def _rwkv_kernel(r_ref, k_ref, v_ref, l_ref, wd_ref, wa_ref, wg_ref, cp_ref, o_ref,
                 st_scr, rst_scr, ov_scr, tr_scr, inj_scr, dec_scr, bon_scr, g_scr, *, group):
    n_streams = r_ref.shape[0]
    ti = pl.program_id(1)
    n_tiles = pl.num_programs(1) - 1
    slot_new = ti % 2
    slot_old = 1 - slot_new

    @pl.when(ti == 0)
    def _():
        st_scr[...] = jnp.zeros_like(st_scr)

    c = CHUNK
    lane = lax.broadcasted_iota(jnp.int32, (1, LANES), 1)
    head0 = lane < HEAD_DIM

    def head_sum(x):
        s0 = jnp.sum(jnp.where(head0, x, 0.0), axis=-1, keepdims=True)
        s1 = jnp.sum(jnp.where(head0, 0.0, x), axis=-1, keepdims=True)
        return jnp.where(head0, s0, s1)

    def bd(x):
        wide = x.shape[1] // LANES
        m0 = head0 if wide == 1 else jnp.concatenate([head0] * wide, axis=1)
        return jnp.concatenate([jnp.where(m0, x, 0.0), jnp.where(m0, 0.0, x)], axis=0)

    def bdot(x, y):
        return jnp.dot(x.astype(BF16), y.astype(BF16), preferred_element_type=F32)

    def bdot_t(x, y):
        return lax.dot_general(x.astype(BF16), y.astype(BF16), (((1,), (1,)), ((), ())),
                               preferred_element_type=F32)

    def tdot(x, y):
        return lax.dot_general(x.astype(BF16), y.astype(BF16), (((0,), (0,)), ((), ())),
                               preferred_element_type=F32)

    def split(x):
        hi = x.astype(BF16)
        return hi, (x - hi.astype(F32)).astype(BF16)

    chunks = lambda x: [x[n * c:(n + 1) * c] for n in range(group)]
    each = lambda f, *xs: [f(*args) for args in zip(*xs)]

    def chunk_terms():
        ri = lax.broadcasted_iota(jnp.int32, (c, 2 * c), 0)
        cj = lax.broadcasted_iota(jnp.int32, (c, 2 * c), 1) % c
        same = lambda size: (ri // size) == (cj // size)
        strict = cj < ri
        incl = cj <= ri
        eye = (ri == cj).astype(F32)
        sizes = []
        size = 4
        while size < c:
            sizes.append(size)
            size *= 2
        inv_masks = (eye, same(4), [same(2 * sz) & jnp.logical_not(same(sz)) for sz in sizes])
        tri = (lax.broadcasted_iota(jnp.int32, (c, c), 1)
               <= lax.broadcasted_iota(jnp.int32, (c, c), 0)).astype(BF16)
        r2 = lax.broadcasted_iota(jnp.int32, (LANES, LANES), 0)
        c2 = lax.broadcasted_iota(jnp.int32, (LANES, LANES), 1)
        same_head = (r2 // HEAD_DIM) == (c2 // HEAD_DIM)
        cp = cp_ref[...]
        w0, a0, k_k, k_a, r_k = (cp[n:n + 1, :] for n in range(5))
        wd = wd_ref[...]
        wa = wa_ref[...]
        wg = wg_ref[...]

        q = {}
        for bi in range(n_streams):
            r = r_ref[bi]
            k = k_ref[bi]
            v = v_ref[bi]
            th = l_ref[bi, :, 0:LANES]
            al = l_ref[bi, :, LANES:2 * LANES]
            sg = l_ref[bi, :, 2 * LANES:]
            z = w0 + bdot(th, wd)
            w = jnp.minimum(z, 0.0) - jnp.log(1.0 + jnp.exp(-jnp.abs(z))) - 0.5
            ld = -jnp.exp(w)
            a = _sigmoid(a0 + bdot(al, wa))
            g = bdot(sg, wg)
            kk = k * k_k
            kk = kk / jnp.maximum(jnp.sqrt(head_sum(kk * kk)), 1e-12)
            kp = k * (1.0 + (a - 1.0) * k_a)
            av = -kk
            bv = kk * a
            bonus = head_sum(r * kp * r_k) * v
            ld_hi, ld_lo = split(ld)
            lc = jnp.concatenate(
                each(lambda hi, lo: (jnp.dot(tri, hi, preferred_element_type=F32)
                                     + jnp.dot(tri, lo, preferred_element_type=F32)),
                     chunks(ld_hi), chunks(ld_lo)), axis=0)
            ltot = [x[c - 1:c, :] for x in chunks(lc)]
            ltot_rows = jnp.concatenate([jnp.broadcast_to(x, (c, LANES)) for x in ltot], axis=0)
            e_nl = jnp.exp(-lc)
            to_end = jnp.exp(ltot_rows - lc)
            vals = dict(rt=chunks(r * jnp.exp(lc)), kt=chunks(kp * e_nl), bt=chunks(bv * e_nl),
                        at=chunks(av * jnp.exp(lc - ld)), kh=chunks(kp * to_end),
                        bh=chunks(bv * to_end), v=chunks(v))
            for name, val in vals.items():
                q.setdefault(name, []).extend(val)
            for n in range(group):
                i = bi * group + n
                dec_scr[slot_new, i] = jnp.broadcast_to(jnp.exp(ltot[n]), (8, LANES))
                bon_scr[slot_new, i] = bonus[n * c:(n + 1) * c]
                g_scr[slot_new, i] = g[n * c:(n + 1) * c]
        yield
        aa = each(lambda a_, r_, b_, k_: bdot_t(jnp.concatenate([a_, r_], axis=0),
                                                jnp.concatenate([bd(b_), bd(k_)], axis=0)),
                  q["at"], q["rt"], q["bt"], q["kt"])
        a_ab = [jnp.where(strict, x[:c, :LANES], 0.0) for x in aa]
        a_ak = [jnp.where(strict, x[:c, LANES:], 0.0) for x in aa]
        a_rb = [jnp.where(incl, x[c:, :LANES], 0.0) for x in aa]
        a_rk = [jnp.where(incl, x[c:, LANES:], 0.0) for x in aa]
        yield
        av = each(lambda a_, v_: bdot(a_, bd(v_)), a_ak, q["v"])
        t_inv = yield from _unit_lower_inverse(a_ab, inv_masks, bd)
        y = each(lambda t_, a_, av_: bdot(t_, bd(jnp.concatenate([a_, av_], axis=1))),
                 t_inv, q["at"], av)
        a_st = [x[:, :LANES] for x in y]
        u_v = [x[:, LANES:] for x in y]
        yield
        zz = each(lambda rb_, rk_, y_, v_: bdot(
            jnp.concatenate([rb_, rk_], axis=1),
            jnp.concatenate([bd(y_), bd(jnp.concatenate([jnp.zeros_like(v_), v_], axis=1))], axis=0)),
            a_rb, a_rk, y, q["v"])
        for i, (rt_, z_) in enumerate(zip(q["rt"], zz)):
            rst_scr[slot_new, i] = rt_ + z_[:, :LANES]
            ov_scr[slot_new, i] = z_[:, LANES:]
        yield
        trans = each(lambda a_, b_: jnp.where(same_head, tdot(a_, b_), 0.0), a_st, q["bh"])
        for i, t_ in enumerate(trans):
            tr_scr[slot_new, i] = t_.astype(BF16)

        def injected(u_, v_, b_, k_):
            full = tdot(jnp.concatenate([u_, v_], axis=0), jnp.concatenate([b_, k_], axis=0))
            return jnp.where(head0, full[:HEAD_DIM], full[HEAD_DIM:])
        for i, x in enumerate(each(injected, u_v, q["v"], q["bh"], q["kh"])):
            inj_scr[slot_new, i] = x

    def recurrence_steps():
        cp = cp_ref[...]
        gn_g, gn_b = cp[5:6, :], cp[6:7, :]
        state = {}

        def step(n):
            if n == 0:
                state["st"] = [st_scr[bi] for bi in range(n_streams)]
            st = state["st"]
            idx = [bi * group + n for bi in range(n_streams)]
            o = [bdot_t(rst_scr[slot_old, i], bd(s)) + ov_scr[slot_old, i] for i, s in zip(idx, st)]
            parts = [split(s) for s in st]
            tb = [tr_scr[slot_old, i] for i in idx]
            hi = [jnp.dot(p_[0], t_, preferred_element_type=F32) for p_, t_ in zip(parts, tb)]
            lo = [jnp.dot(p_[1], t_, preferred_element_type=F32) for p_, t_ in zip(parts, tb)]
            st = [s * dec_scr[slot_old, i, 0:1, :] + h_ + l_ + inj_scr[slot_old, i]
                  for s, i, h_, l_ in zip(st, idx, hi, lo)]
            state["st"] = st
            for bi, i in enumerate(idx):
                mean = head_sum(o[bi]) * (1.0 / HEAD_DIM)
                dev = o[bi] - mean
                var = head_sum(dev * dev) * (1.0 / HEAD_DIM)
                on = dev * lax.rsqrt(var + GN_EPS) * gn_g + gn_b
                o_ref[bi, n * c:(n + 1) * c, :] = (
                    (on + bon_scr[slot_old, i]) * g_scr[slot_old, i]).astype(o_ref.dtype)
            if n == group - 1:
                for bi in range(n_streams):
                    st_scr[bi] = st[bi]
        return [functools.partial(step, n) for n in range(group)]

    def run(with_terms, with_recurrence):
        steps = recurrence_steps() if with_recurrence else []
        if with_terms:
            for _ in chunk_terms():
                if steps:
                    steps.pop(0)()
        for s in steps:
            s()

    @pl.when(ti == 0)
    def _():
        run(True, False)

    @pl.when((ti > 0) & (ti < n_tiles))
    def _():
        run(True, True)

    @pl.when(ti == n_tiles)
    def _():
        run(False, True)


def _rwkv(p3, l3, wd, wa, wg, cp, *, n_pairs, group):
    b, s, _ = p3.shape
    col0 = 3 * n_pairs
    tt = group * CHUNK
    n_tiles = s // tt
    n_terms = b * group
    tile_in = lambda ti: jnp.minimum(ti, n_tiles - 1)

    def pspec(off):
        return pl.BlockSpec((b, tt, LANES), lambda hp, ti: (0, tile_in(ti), col0 + off + hp))

    wspec = lambda rows: pl.BlockSpec((rows, LANES), lambda hp, ti: (0, hp))
    terms = lambda rows, dt: pltpu.VMEM((2, n_terms, rows, LANES), dt)
    kern = functools.partial(_rwkv_kernel, group=group)
    return pl.pallas_call(
        kern,
        grid=(n_pairs, n_tiles + 1),
        in_specs=[pspec(0), pspec(n_pairs), pspec(2 * n_pairs),
                  pl.BlockSpec((b, tt, LORA_PAD), lambda hp, ti: (0, tile_in(ti), 0)),
                  wspec(LANES), wspec(LANES), wspec(2 * LANES), wspec(8)],
        out_specs=pl.BlockSpec((b, tt, LANES), lambda hp, ti: (0, jnp.maximum(ti - 1, 0), hp)),
        out_shape=jax.ShapeDtypeStruct((b, s, n_pairs * LANES), BF16),
        scratch_shapes=[pltpu.VMEM((b, HEAD_DIM, LANES), F32),
                        terms(CHUNK, F32), terms(CHUNK, F32), terms(LANES, BF16),
                        terms(HEAD_DIM, F32), terms(8, F32), terms(CHUNK, F32), terms(CHUNK, F32)],
        compiler_params=_cparams(("arbitrary", "arbitrary")),
        name="rwkv",
    )(p3, p3, p3, l3, wd, wa, wg, cp)


def _mixffn_kernel(a_ref, r_ref, wa_ref, wr_ref, x_ref, gt1_ref, sc_ref, sh_ref, gt2_ref, g2_ref,
                   gf_ref, wg_ref, wu_ref, wd_ref, o_ref, x1_scr, h_scr, acc_scr):
    f = pl.program_id(1)

    def rms(x, g):
        return x * lax.rsqrt(jnp.mean(x * x, axis=-1, keepdims=True) + RMS_EPS) * g

    @pl.when(f == 0)
    def _():
        mix = jnp.dot(a_ref[...], wa_ref[...], preferred_element_type=F32)
        mix += jnp.dot(r_ref[...], wr_ref[...], preferred_element_type=F32)
        x1 = x_ref[...] + gt1_ref[...] * mix
        x1_scr[...] = x1
        h = rms(x1, g2_ref[...]) * (1.0 + sc_ref[...]) + sh_ref[...]
        h_scr[...] = h.astype(BF16)
        acc_scr[...] = jnp.zeros_like(acc_scr)

    h = h_scr[...]
    gate = jnp.dot(h, wg_ref[...], preferred_element_type=F32)
    up = jnp.dot(h, wu_ref[...], preferred_element_type=F32)
    act = (gate * _sigmoid(gate) * up).astype(BF16)
    acc_scr[...] += jnp.dot(act, wd_ref[...], preferred_element_type=F32)

    @pl.when(f == pl.num_programs(1) - 1)
    def _():
        y = x1_scr[...] + gt2_ref[...] * acc_scr[...]
        o_ref[...] = rms(y, gf_ref[...])


def _mixffn(attn, rw, w_a, w_r, x2, gt1, sc2, sh2, gt2, g2, gf, w_gate, w_up, w_down,
            *, seq, tm, tf):
    m, d = x2.shape
    half = attn.shape[1]
    fdim = w_gate.shape[1]
    tiles_per_batch = seq // tm
    once = pl.Buffered(1)
    vec = pl.BlockSpec((None, 1, d), lambda i, f: (i // tiles_per_batch, 0, 0))
    par = pl.BlockSpec((1, d), lambda i, f: (0, 0))
    return pl.pallas_call(
        _mixffn_kernel,
        grid=(m // tm, fdim // tf),
        in_specs=[
            pl.BlockSpec((tm, half), lambda i, f: (i, 0)),
            pl.BlockSpec((tm, half), lambda i, f: (i, 0)),
            pl.BlockSpec((half, d), lambda i, f: (0, 0), pipeline_mode=once),
            pl.BlockSpec((half, d), lambda i, f: (0, 0), pipeline_mode=once),
            pl.BlockSpec((tm, d), lambda i, f: (i, 0)),
            vec, vec, vec, vec, par, par,
            pl.BlockSpec((d, tf), lambda i, f: (0, f)),
            pl.BlockSpec((d, tf), lambda i, f: (0, f)),
            pl.BlockSpec((tf, d), lambda i, f: (f, 0)),
        ],
        out_specs=pl.BlockSpec((tm, d), lambda i, f: (i, 0)),
        out_shape=jax.ShapeDtypeStruct((m, d), F32),
        scratch_shapes=[pltpu.VMEM((tm, d), F32), pltpu.VMEM((tm, d), BF16),
                        pltpu.VMEM((tm, d), F32)],
        compiler_params=_cparams(("arbitrary", "arbitrary"), MIXFFN_VMEM_LIMIT),
        name="mixffn",
    )(attn, rw, w_a, w_r, x2, gt1, sc2, sh2, gt2, g2, gf, w_gate, w_up, w_down)


def _largest_tile(total, cap, step):
    t = min(total, cap)
    t -= t % step
    while total % t:
        t -= step
    return t


def kernel(x, c, positions, w_ada, b_ada, norm1_g, norm2_g, normf_g, w_in, w_out, mu_shift, w0,
           w_decay_up, a0, w_iclr_up, w_gate_up, k_k, k_a, r_k, gn_g, gn_b, w_ffn_gate,
           w_ffn_up, w_ffn_down):
    b, s, d = x.shape
    depth = w_ada.shape[0]
    assert depth == 1, "the final norm is fused into the (single) channel mixer"
    half = d // 2
    n_pairs = half // LANES
    m = b * s
    assert s % SPAN == 0 and half % LANES == 0
    tm = _largest_tile(s, 512, 8)
    assert s % (RWKV_GROUP * CHUNK) == 0
    fdim = w_ffn_gate.shape[-1]
    tf = _largest_tile(fdim, 512, LANES)

    posb = jnp.broadcast_to(positions.astype(F32).reshape(m, 1), (m, LANES))
    x2 = x.reshape(m, d)

    def lora_cols(t, axis):
        parts = jnp.split(t, [DECAY_LORA, DECAY_LORA + ICLR_LORA], axis=axis)
        offs = (0, LANES, 2 * LANES)
        shape = list(t.shape)
        shape[axis] = LORA_PAD
        out = jnp.zeros(shape, t.dtype)
        for part, off in zip(parts, offs):
            idx = [slice(None)] * t.ndim
            idx[axis] = slice(off, off + part.shape[axis])
            out = out.at[tuple(idx)].set(part)
        return out

    def pad_rows(t, rows):
        return jnp.pad(t, ((0, rows - t.shape[0]), (0, 0)))

    for i in range(depth):
        ada = _ada(c, w_ada[i], b_ada[i][None, :])
        sh1, sc1, gt1, sh2, sc2, gt2 = (t[:, None, :] for t in jnp.split(ada, 6, axis=-1))

        w_all = w_in[i].astype(BF16)
        w_lora = lora_cols(w_in[i][:, 6 * half:], 1).astype(BF16)
        mu_main = mu_shift[i][None, :3 * half]
        mu_lora = lora_cols(mu_shift[i][None, 3 * half:], 1)
        p2, l2 = _inproj(x2, posb, sc1, sh1, norm1_g[i][None, :], w_all, w_lora,
                         mu_main, mu_lora, seq=s, tm=_largest_tile(s, 256, 8))
        p3 = p2.reshape(b, s, 6 * half)
        l3 = l2.reshape(b, s, LORA_PAD)

        attn = _attention(p3, n_pairs=n_pairs)

        cp = jnp.stack([w0[i], a0[i], k_k[i], k_a[i], r_k[i].reshape(-1), gn_g[i], gn_b[i],
                        jnp.zeros_like(w0[i])])
        rw = _rwkv(p3, l3, pad_rows(w_decay_up[i], LANES), pad_rows(w_iclr_up[i], LANES),
                   pad_rows(w_gate_up[i], 2 * LANES), cp, n_pairs=n_pairs, group=RWKV_GROUP)

        wo = w_out[i].astype(BF16)
        x2 = _mixffn(attn.reshape(m, half), rw.reshape(m, half), wo[:half], wo[half:], x2, gt1,
                     sc2, sh2, gt2, norm2_g[i][None, :], normf_g[None, :],
                     w_ffn_gate[i].astype(BF16), w_ffn_up[i].astype(BF16),
                     w_ffn_down[i].astype(BF16), seq=s, tm=tm, tf=tf)
    return x2.reshape(b, s, d)
```
